```python
import math
import jax, jax.numpy as jnp
from jax import lax
import numpy as np

D_MODEL = 1024
BATCH = 4
SEQ = 4096
DEPTH = 1

CTX_LEN = 256
GRID_W = 64
ROPE_BASE = 10000.0
NORM_EPS = 1e-6
Q_BLOCK = 128

MLA_HEADS = 8
MLA_NOPE = 64
MLA_ROPE = 32
MLA_V = 64
MLA_Q_RANK = 256
MLA_KV_RANK = 128
DIFF_HEADS = 4
DIFF_QK = 64
DIFF_V = 2 * DIFF_QK

MIX_WIDTH = MLA_HEADS * MLA_V + DIFF_HEADS * DIFF_V
IN_SPLITS = (
    MLA_Q_RANK,
    MLA_Q_RANK + MLA_KV_RANK,
    MLA_Q_RANK + MLA_KV_RANK + MLA_ROPE,
    MLA_Q_RANK + MLA_KV_RANK + MLA_ROPE + DIFF_HEADS * 2 * DIFF_QK,
    MLA_Q_RANK + MLA_KV_RANK + MLA_ROPE + 2 * DIFF_HEADS * 2 * DIFF_QK,
)
IN_COLS = IN_SPLITS[-1] + DIFF_HEADS * DIFF_V

N_EXPERTS = 256
TOP_K = 8
N_GROUPS = 8
TOPK_GROUPS = 4
EXPERT_FF = 256
SHARED_FF = 256
ROUTED_SCALE = 2.5
MOE_BLOCK = 128

kernel_name = 'hybrid_mla_diffattn_moe_dit'


def _rms(x, g):
    xf = x.astype(jnp.float32)
    y = xf * lax.rsqrt(jnp.mean(xf * xf, axis=-1, keepdims=True) + NORM_EPS)
    return (y * g.astype(jnp.float32)).astype(x.dtype)


def _modulate(h, shift, scale):
    return h * (1 + scale) + shift


def _axial_tables(n_ctx, rows, rot_dim):
    n_freq = rot_dim // 4
    inv = ROPE_BASE ** (-(jnp.arange(n_freq, dtype=jnp.float32) / n_freq))
    row = jnp.repeat(jnp.arange(rows, dtype=jnp.float32), GRID_W)
    col = jnp.tile(jnp.arange(GRID_W, dtype=jnp.float32), rows)
    theta = jnp.concatenate([row[:, None] * inv, col[:, None] * inv], axis=-1)
    theta = jnp.concatenate([jnp.zeros((n_ctx, 2 * n_freq), jnp.float32), theta], axis=0)
    return jnp.cos(theta), jnp.sin(theta)


def _rope(x, cos, sin):
    half = x.shape[-1] // 2
    x1, x2 = x[..., :half], x[..., half:]
    cos = cos.astype(x.dtype)
    sin = sin.astype(x.dtype)
    return jnp.concatenate([x1 * cos - x2 * sin, x1 * sin + x2 * cos], axis=-1)


def _sweep(fn, qs):
    def to_blocks(q):
        b, h, s, d = q.shape
        return q.reshape(b, h, s // Q_BLOCK, Q_BLOCK, d).transpose(2, 0, 1, 3, 4)
    out = lax.map(lambda args: fn(*args), tuple(to_blocks(q) for q in qs))
    nb, b, h, qb, dv = out.shape
    return out.transpose(1, 2, 0, 3, 4).reshape(b, h, nb * qb, dv)


def _attend(q, k, v, scale):
    s = jnp.einsum('bhqd,bhkd->bhqk', q, k, preferred_element_type=jnp.float32) * scale
    p = jax.nn.softmax(s, axis=-1)
    return jnp.einsum('bhqk,bhkd->bhqd', p.astype(v.dtype), v)


def _diff_attend(q1, q2, k1, k2, v, lam, scale):
    s1 = jnp.einsum('bhqd,bhkd->bhqk', q1, k1, preferred_element_type=jnp.float32) * scale
    s2 = jnp.einsum('bhqd,bhkd->bhqk', q2, k2, preferred_element_type=jnp.float32) * scale
    p = jax.nn.softmax(s1, axis=-1) - lam * jax.nn.softmax(s2, axis=-1)
    return jnp.einsum('bhqk,bhkd->bhqd', p.astype(v.dtype), v)


def _merge_heads(o):
    b, h, s, d = o.shape
    return o.transpose(0, 2, 1, 3).reshape(b, s, h * d)


def _mixer(h, n_ctx, lam_init, w_in, g_q, w_uq, g_kv, w_ukv, lq1, lk1, lq2, lk2, g_sub, w_out,
           cos_m, sin_m, cos_d, sin_d, with_ctx):
    b, n, _ = h.shape
    p = h @ w_in
    c_q, c_kv, k_r, dq, dk, dv = jnp.split(p, IN_SPLITS, axis=-1)

    q_m = (_rms(c_q, g_q) @ w_uq).reshape(b, n, MLA_HEADS, MLA_NOPE + MLA_ROPE).transpose(0, 2, 1, 3)
    q_m = jnp.concatenate([q_m[..., :MLA_NOPE], _rope(q_m[..., MLA_NOPE:], cos_m, sin_m)], axis=-1)
    kv = (_rms(c_kv, g_kv) @ w_ukv).reshape(b, n, MLA_HEADS, MLA_NOPE + MLA_V).transpose(0, 2, 1, 3)
    k_nope, v_m = kv[..., :MLA_NOPE], kv[..., MLA_NOPE:]
    k_r = _rope(k_r, cos_m, sin_m)
    k_m = jnp.concatenate([k_nope, jnp.broadcast_to(k_r[:, None], (b, MLA_HEADS, n, MLA_ROPE))], axis=-1)
    scale_m = 1.0 / math.sqrt(MLA_NOPE + MLA_ROPE)

    dq = _rope(dq.reshape(b, n, DIFF_HEADS, 2, DIFF_QK).transpose(0, 2, 3, 1, 4), cos_d, sin_d)
    dk = _rope(dk.reshape(b, n, DIFF_HEADS, 2, DIFF_QK).transpose(0, 2, 3, 1, 4), cos_d, sin_d)
    q1, q2 = dq[:, :, 0], dq[:, :, 1]
    k1, k2 = dk[:, :, 0], dk[:, :, 1]
    v_d = dv.reshape(b, n, DIFF_HEADS, DIFF_V).transpose(0, 2, 1, 3)
    lam = (jnp.exp(jnp.sum(lq1.astype(jnp.float32) * lk1.astype(jnp.float32)))
           - jnp.exp(jnp.sum(lq2.astype(jnp.float32) * lk2.astype(jnp.float32))) + lam_init)
    scale_d = 1.0 / math.sqrt(DIFF_QK)

    def combine(o_m, o_d):
        o_d = _rms(o_d, g_sub) * (1.0 - lam_init)
        return jnp.concatenate([_merge_heads(o_m), _merge_heads(o_d)], axis=-1) @ w_out

    o_m_lat = _sweep(lambda qb: _attend(qb, k_m, v_m, scale_m), (q_m[:, :, n_ctx:],))
    o_d_lat = _sweep(lambda a, c: _diff_attend(a, c, k1, k2, v_d, lam, scale_d),
                     (q1[:, :, n_ctx:], q2[:, :, n_ctx:]))
    y_lat = combine(o_m_lat, o_d_lat)
    y_ctx = None
    if with_ctx:
        o_m_ctx = _attend(q_m[:, :, :n_ctx], k_m[:, :, :n_ctx], v_m[:, :, :n_ctx], scale_m)
        o_d_ctx = _diff_attend(q1[:, :, :n_ctx], q2[:, :, :n_ctx], k1[:, :, :n_ctx], k2[:, :, :n_ctx],
                               v_d[:, :, :n_ctx], lam, scale_d)
        y_ctx = combine(o_m_ctx, o_d_ctx)
    return y_lat, y_ctx


def _swiglu(x, w_gate, w_up, w_down):
    return (jax.nn.silu(x @ w_gate) * (x @ w_up)) @ w_down


def _grouped_experts(hf, e_idx, w, w1, w3, w2):
    t, k = e_idx.shape
    n_assign = t * k
    n_exp = w1.shape[0]
    n_blocks = -(-(n_assign + n_exp * (MOE_BLOCK - 1)) // MOE_BLOCK)
    n_slots = n_blocks * MOE_BLOCK
    e_flat = e_idx.reshape(n_assign)
    tok_flat = jnp.repeat(jnp.arange(t, dtype=jnp.int32), k)
    w_flat = w.reshape(n_assign)
    order = jnp.argsort(e_flat)
    e_sorted = e_flat[order]
    counts = jnp.bincount(e_flat, length=n_exp)
    start = jnp.cumsum(counts) - counts
    padded = (counts + MOE_BLOCK - 1) // MOE_BLOCK * MOE_BLOCK
    pend = jnp.cumsum(padded)
    pstart = pend - padded
    dest = pstart[e_sorted] + (jnp.arange(n_assign, dtype=jnp.int32) - start[e_sorted])
    slot_tok = jnp.zeros((n_slots,), jnp.int32).at[dest].set(tok_flat[order])
    slot_w = jnp.zeros((n_slots,), w.dtype).at[dest].set(w_flat[order])
    blk_e = jnp.minimum(jnp.searchsorted(pend, jnp.arange(n_blocks, dtype=pend.dtype) * MOE_BLOCK, side='right'),
                        n_exp - 1)

    def block(args):
        tok, wt, e = args
        return _swiglu(hf[tok], w1[e], w3[e], w2[e]) * wt[:, None]

    y = lax.map(block, (slot_tok.reshape(n_blocks, MOE_BLOCK), slot_w.reshape(n_blocks, MOE_BLOCK), blk_e))
    return jax.ops.segment_sum(y.reshape(n_slots, -1), slot_tok, num_segments=t)


def _moe(h, w_r, b_r, w1, w3, w2, ws1, ws3, ws2):
    b, n, d = h.shape
    t = b * n
    hf = h.reshape(t, d)
    s = jax.nn.sigmoid(jnp.einsum('td,de->te', hf, w_r, preferred_element_type=jnp.float32))
    s_sel = s + b_r.astype(jnp.float32)
    per_group = N_EXPERTS // N_GROUPS
    grp_score = lax.top_k(s_sel.reshape(t, N_GROUPS, per_group), 2)[0].sum(-1)
    _, g_idx = lax.top_k(grp_score, TOPK_GROUPS)
    g_mask = jax.nn.one_hot(g_idx, N_GROUPS, dtype=jnp.float32).sum(1) > 0
    e_mask = jnp.repeat(g_mask, per_group, axis=1)
    _, e_idx = lax.top_k(jnp.where(e_mask, s_sel, -jnp.inf), TOP_K)
    wts = jnp.take_along_axis(s, e_idx, axis=1)
    wts = wts / jnp.sum(wts, axis=-1, keepdims=True) * ROUTED_SCALE
    routed = _grouped_experts(hf, e_idx, wts.astype(h.dtype), w1, w3, w2)
    shared = _swiglu(hf, ws1, ws3, ws2)
    return (routed + shared).reshape(b, n, d)


def setup_inputs(seed: int = 0) -> dict:
    key = jax.random.key(seed)
    ks = jax.random.split(key, 32)
    L, D = DEPTH, D_MODEL

    def nrm(k, shape, scale):
        return jax.random.normal(k, shape, jnp.float32) * scale

    return {
        'x': nrm(ks[0], (BATCH, SEQ, D), 1.0),
        'c': nrm(ks[1], (BATCH, D), 1.0),
        'ctx': nrm(ks[2], (BATCH, CTX_LEN, D), 1.0),
        'c_ctx': nrm(ks[3], (D,), 1.0),
        'w_mod': nrm(ks[4], (L, D, 6 * D), 0.5 * D ** -0.5),
        'b_mod': nrm(ks[5], (L, 6 * D), 0.02),
        'g_attn': 1.0 + nrm(ks[6], (L, D), 0.02),
        'g_ffn': 1.0 + nrm(ks[7], (L, D), 0.02),
        'w_in': nrm(ks[8], (L, D, IN_COLS), D ** -0.5),
        'g_q_lat': 1.0 + nrm(ks[9], (L, MLA_Q_RANK), 0.02),
        'w_uq': nrm(ks[10], (L, MLA_Q_RANK, MLA_HEADS * (MLA_NOPE + MLA_ROPE)), MLA_Q_RANK ** -0.5),
        'g_kv_lat': 1.0 + nrm(ks[11], (L, MLA_KV_RANK), 0.02),
        'w_ukv': nrm(ks[12], (L, MLA_KV_RANK, MLA_HEADS * (MLA_NOPE + MLA_V)), MLA_KV_RANK ** -0.5),
        'lam_q1': nrm(ks[13], (L, DIFF_QK), 0.1),
        'lam_k1': nrm(ks[14], (L, DIFF_QK), 0.1),
        'lam_q2': nrm(ks[15], (L, DIFF_QK), 0.1),
        'lam_k2': nrm(ks[16], (L, DIFF_QK), 0.1),
        'g_subln': 1.0 + nrm(ks[17], (L, DIFF_V), 0.02),
        'w_out': nrm(ks[18], (L, MIX_WIDTH, D), MIX_WIDTH ** -0.5),
        'w_router': nrm(ks[19], (L, D, N_EXPERTS), D ** -0.5),
        'router_bias': nrm(ks[20], (L, N_EXPERTS), 0.01),
        'w1': nrm(ks[21], (L, N_EXPERTS, D, EXPERT_FF), D ** -0.5),
        'w3': nrm(ks[22], (L, N_EXPERTS, D, EXPERT_FF), D ** -0.5),
        'w2': nrm(ks[23], (L, N_EXPERTS, EXPERT_FF, D), EXPERT_FF ** -0.5),
        'ws1': nrm(ks[24], (L, D, SHARED_FF), D ** -0.5),
        'ws3': nrm(ks[25], (L, D, SHARED_FF), D ** -0.5),
        'ws2': nrm(ks[26], (L, SHARED_FF, D), SHARED_FF ** -0.5),
        'g_final': 1.0 + nrm(ks[27], (D,), 0.02),
    }


def reference(x, c, ctx, c_ctx, w_mod, b_mod, g_attn, g_ffn, w_in, g_q_lat, w_uq, g_kv_lat, w_ukv,
              lam_q1, lam_k1, lam_q2, lam_k2, g_subln, w_out, w_router, router_bias, w1, w3, w2,
              ws1, ws3, ws2, g_final):
    n_lat = x.shape[1]
    n_ctx = ctx.shape[1]
    rows = n_lat // GRID_W
    cos_m, sin_m = _axial_tables(n_ctx, rows, MLA_ROPE)
    cos_d, sin_d = _axial_tables(n_ctx, rows, DIFF_QK)
    xc = ctx
    for l in range(DEPTH):
        with_ctx = l < DEPTH - 1
        lam_init = 0.8 - 0.6 * math.exp(-0.3 * l)
        mod_x = jnp.split(jax.nn.silu(c) @ w_mod[l] + b_mod[l], 6, axis=-1)
        mod_c = jnp.split(jax.nn.silu(c_ctx) @ w_mod[l] + b_mod[l], 6, axis=-1)
        sh1, sc1, gt1, sh2, sc2, gt2 = [m[:, None, :] for m in mod_x]
        csh1, csc1, cgt1, csh2, csc2, cgt2 = mod_c

        h = jnp.concatenate([_modulate(_rms(xc, g_attn[l]), csh1, csc1),
                             _modulate(_rms(x, g_attn[l]), sh1, sc1)], axis=1)
        y_lat, y_ctx = _mixer(h, n_ctx, lam_init, w_in[l], g_q_lat[l], w_uq[l], g_kv_lat[l], w_ukv[l],
                              lam_q1[l], lam_k1[l], lam_q2[l], lam_k2[l], g_subln[l], w_out[l],
                              cos_m, sin_m, cos_d, sin_d, with_ctx)
        x = x + gt1 * y_lat
        x = x + gt2 * _moe(_modulate(_rms(x, g_ffn[l]), sh2, sc2), w_router[l], router_bias[l],
                           w1[l], w3[l], w2[l], ws1[l], ws3[l], ws2[l])
        if with_ctx:
            xc = xc + cgt1 * y_ctx
            xc = xc + cgt2 * _moe(_modulate(_rms(xc, g_ffn[l]), csh2, csc2), w_router[l], router_bias[l],
                                  w1[l], w3[l], w2[l], ws1[l], ws3[l], ws2[l])
    return _rms(x, g_final)
```

```python
import functools
import math

import jax
import jax.numpy as jnp
from jax import lax
from jax.experimental import pallas as pl
from jax.experimental.pallas import tpu as pltpu

GRID_W = 64
ROPE_BASE = 10000.0
NORM_EPS = 1e-6
MLA_HEADS = 8
MLA_NOPE = 64
MLA_ROPE = 32
MLA_V = 64
MLA_Q_RANK = 256
MLA_KV_RANK = 128
DIFF_HEADS = 4
DIFF_QK = 64
DIFF_V = 2 * DIFF_QK
N_EXPERTS = 256
TOP_K = 8
N_GROUPS = 8
TOPK_GROUPS = 4
ROUTED_SCALE = 2.5
LAM_INIT = 0.8 - 0.6 * math.exp(-0.3 * 0)

LANES = 128
VMEM_LIMIT = 48 * 1024 * 1024

F32 = jnp.float32
BF16 = jnp.bfloat16
NEG_INF = float("-inf")


def _cparams(sem):
    return pltpu.CompilerParams(dimension_semantics=sem, vmem_limit_bytes=VMEM_LIMIT)


def _rms_rows(x, g):
    return x * lax.rsqrt(jnp.mean(x * x, axis=-1, keepdims=True) + NORM_EPS) * g


def _dot(a, b):
    return jnp.dot(a, b, preferred_element_type=F32)


def _dot_nt(a, b):
    return lax.dot_general(a, b, (((1,), (1,)), ((), ())), preferred_element_type=F32)


def _mod_kernel(c_ref, w_ref, b_ref, o_ref):
    a = c_ref[...]
    a = a * jax.nn.sigmoid(a)
    o_ref[...] = jnp.dot(a, w_ref[...], preferred_element_type=F32,
                         precision=lax.Precision.HIGHEST) + b_ref[...]


def _modulation(cc, w_mod, b_mod):
    rows, d = cc.shape
    cols = w_mod.shape[1]
    tn = 1536
    return pl.pallas_call(
        _mod_kernel,
        out_shape=jax.ShapeDtypeStruct((rows, cols), F32),
        grid=(cols // tn,),
        in_specs=[pl.BlockSpec((rows, d), lambda j: (0, 0)),
                  pl.BlockSpec((d, tn), lambda j: (0, j)),
                  pl.BlockSpec((1, tn), lambda j: (0, j))],
        out_specs=pl.BlockSpec((rows, tn), lambda j: (0, j)),
        compiler_params=_cparams(("arbitrary",)),
        name="mod",
    )(cc, w_mod, b_mod)


_C_Q = 0
_C_KV = _C_Q + MLA_Q_RANK
_C_DQ = _C_KV + MLA_KV_RANK
_C_DQS = _C_DQ + DIFF_HEADS * 2 * DIFF_QK
_C_DK = _C_DQS + DIFF_HEADS * 2 * DIFF_QK
_C_DKS = _C_DK + DIFF_HEADS * 2 * DIFF_QK
_C_DV = _C_DKS + DIFF_HEADS * 2 * DIFF_QK
_C_KRA = _C_DV + DIFF_HEADS * DIFF_V
_C_KRB = _C_KRA + LANES
_IN_EXT = _C_KRB + LANES
_N_TAB = 10


def _proj_kernel(n_batch, d_model, ctx_ref, x_ref, mod_ref, g_attn_ref, w_in_ref, g_q_ref, w_uq_ref,
                 g_kv_ref, w_ukv_ref, tab_ref,
                 qm_ref, km_ref, vm_ref, q1_ref, q2_ref, k12_ref, vd_ref, *, n_ctx_tiles):
    b = pl.program_id(0)
    i = pl.program_id(1)
    is_ctx = i < n_ctx_tiles
    xin = jnp.where(is_ctx, ctx_ref[0], x_ref[0])
    row = jnp.where(is_ctx, n_batch, b)
    sh1 = mod_ref[pl.ds(row, 1), 0:d_model]
    sc1 = mod_ref[pl.ds(row, 1), d_model:2 * d_model]
    h = _rms_rows(xin, g_attn_ref[...]) * (1.0 + sc1) + sh1
    p = _dot(h.astype(BF16), w_in_ref[...])

    cq = _rms_rows(p[:, _C_Q:_C_Q + MLA_Q_RANK], g_q_ref[...])
    qa = _dot(cq.astype(BF16), w_uq_ref[...])
    ckv = _rms_rows(p[:, _C_KV:_C_KV + MLA_KV_RANK], g_kv_ref[...])
    kv = _dot(ckv.astype(BF16), w_ukv_ref[...])

    def tab(j):
        return tab_ref[:, j * LANES:(j + 1) * LANES]

    kr = p[:, _C_KRA:_C_KRA + LANES] * tab(2) + p[:, _C_KRB:_C_KRB + LANES] * tab(3)
    hw = MLA_HEADS * LANES
    for hd in range(MLA_HEADS):
        lo = hd * LANES
        qm_ref[0, hd] = (qa[:, lo:lo + LANES] * tab(0) + qa[:, hw + lo:hw + lo + LANES] * tab(1)).astype(BF16)
        km_ref[0, hd] = (kv[:, lo:lo + LANES] + kr).astype(BF16)
        vm_ref[0, hd] = kv[:, hw + lo:hw + lo + LANES].astype(BF16)
    for hd in range(DIFF_HEADS):
        lo = hd * LANES
        dq = p[:, _C_DQ + lo:_C_DQ + lo + LANES]
        dqs = p[:, _C_DQS + lo:_C_DQS + lo + LANES]
        q1_ref[0, hd] = (dq * tab(4) + dqs * tab(5)).astype(BF16)
        q2_ref[0, hd] = (dq * tab(6) + dqs * tab(7)).astype(BF16)
        dk = p[:, _C_DK + lo:_C_DK + lo + LANES]
        dks = p[:, _C_DKS + lo:_C_DKS + lo + LANES]
        k12_ref[0, hd] = (dk * tab(8) + dks * tab(9)).astype(BF16)
        vd_ref[0, hd] = p[:, _C_DV + lo:_C_DV + lo + LANES].astype(BF16)


def _projections(ctx, x, mod, g_attn, w_in_ext, g_q, w_uq_ext, g_kv, w_ukv_ext, tables, tm):
    n_batch, n_ctx, d = ctx.shape
    n_lat = x.shape[1]
    n_all = n_ctx + n_lat
    nct = n_ctx // tm
    nt = n_all // tm

    def full(a):
        return pl.BlockSpec(a.shape, lambda b, i: (0,) * a.ndim)

    def kv_spec(heads):
        return pl.BlockSpec((1, heads, tm, LANES), lambda b, i: (b, 0, i, 0))

    def q_spec(heads):
        return pl.BlockSpec((1, heads, tm, LANES), lambda b, i: (b, 0, jnp.maximum(i - nct, 0), 0))

    def sds(heads, n):
        return jax.ShapeDtypeStruct((n_batch, heads, n, LANES), BF16)

    return pl.pallas_call(
        functools.partial(_proj_kernel, n_batch, d, n_ctx_tiles=nct),
        out_shape=(sds(MLA_HEADS, n_lat), sds(MLA_HEADS, n_all), sds(MLA_HEADS, n_all),
                   sds(DIFF_HEADS, n_lat), sds(DIFF_HEADS, n_lat), sds(DIFF_HEADS, n_all),
                   sds(DIFF_HEADS, n_all)),
        grid=(n_batch, nt),
        in_specs=[pl.BlockSpec((1, tm, d), lambda b, i: (b, jnp.minimum(i, nct - 1), 0)),
                  pl.BlockSpec((1, tm, d), lambda b, i: (b, jnp.maximum(i - nct, 0), 0)),
                  full(mod), full(g_attn), full(w_in_ext), full(g_q), full(w_uq_ext), full(g_kv),
                  full(w_ukv_ext),
                  pl.BlockSpec((tm, _N_TAB * LANES), lambda b, i: (i, 0))],
        out_specs=(q_spec(MLA_HEADS), kv_spec(MLA_HEADS), kv_spec(MLA_HEADS),
                   q_spec(DIFF_HEADS), q_spec(DIFF_HEADS), kv_spec(DIFF_HEADS), kv_spec(DIFF_HEADS)),
        compiler_params=_cparams(("arbitrary", "arbitrary")),
        name="proj",
    )(ctx, x, mod, g_attn, w_in_ext, g_q, w_uq_ext, g_kv, w_ukv_ext, tables)


def _softmax_parts(s):
    m = jnp.max(s, axis=-1, keepdims=True)
    e = jnp.exp(s - m)
    return e, jnp.sum(e, axis=-1, keepdims=True)


def _attn_mla_kernel(q_ref, k_ref, v_ref, o_ref):
    acc = None
    for j in range(2):
        e, l = _softmax_parts(_dot_nt(q_ref[0, j], k_ref[0, j]))
        o = _dot(e.astype(BF16), v_ref[0, j]) * (1.0 / l)
        acc = o if acc is None else acc + o
    o_ref[0] = acc.astype(BF16)


def _attn_mla(qm, km, vm, tq):
    n_batch, heads, n_lat, _ = qm.shape
    n_all = km.shape[2]
    return pl.pallas_call(
        _attn_mla_kernel,
        out_shape=jax.ShapeDtypeStruct((n_batch, n_lat, heads // 2 * LANES), BF16),
        grid=(n_batch, heads // 2, n_lat // tq),
        in_specs=[pl.BlockSpec((1, 2, tq, LANES), lambda b, h, i: (b, h, i, 0)),
                  pl.BlockSpec((1, 2, n_all, LANES), lambda b, h, i: (b, h, 0, 0)),
                  pl.BlockSpec((1, 2, n_all, LANES), lambda b, h, i: (b, h, 0, 0))],
        out_specs=pl.BlockSpec((1, tq, LANES), lambda b, h, i: (b, i, h)),
        compiler_params=_cparams(("arbitrary", "arbitrary", "arbitrary")),
        name="attn_m",
    )(qm, km, vm)


def _attn_diff_kernel(lq1_ref, lk1_ref, lq2_ref, lk2_ref, g_sub_ref, q1_ref, q2_ref, k_ref, v_ref, o_ref):
    lam = (jnp.exp(jnp.sum(lq1_ref[...] * lk1_ref[...], axis=-1, keepdims=True))
           - jnp.exp(jnp.sum(lq2_ref[...] * lk2_ref[...], axis=-1, keepdims=True)) + LAM_INIT)
    k = k_ref[0, 0]
    e1, l1 = _softmax_parts(_dot_nt(q1_ref[0, 0], k))
    e2, l2 = _softmax_parts(_dot_nt(q2_ref[0, 0], k))
    p = e1 * (1.0 / l1) - e2 * (lam / l2)
    o = _dot(p.astype(BF16), v_ref[0, 0])
    o_ref[0] = (_rms_rows(o, g_sub_ref[...]) * (1.0 - LAM_INIT)).astype(BF16)


def _attn_diff(lams, g_sub, q1, q2, k12, vd, tq):
    n_batch, heads, n_lat, _ = q1.shape
    n_all = k12.shape[2]

    def small(a):
        return pl.BlockSpec(a.shape, lambda b, h, i: (0,) * a.ndim)

    q_spec = pl.BlockSpec((1, 1, tq, LANES), lambda b, h, i: (b, h, i, 0))
    kv_spec = pl.BlockSpec((1, 1, n_all, LANES), lambda b, h, i: (b, h, 0, 0))
    return pl.pallas_call(
        _attn_diff_kernel,
        out_shape=jax.ShapeDtypeStruct((n_batch, n_lat, heads * LANES), BF16),
        grid=(n_batch, heads, n_lat // tq),
        in_specs=[small(lams[0]), small(lams[1]), small(lams[2]), small(lams[3]), small(g_sub),
                  q_spec, q_spec, kv_spec, kv_spec],
        out_specs=pl.BlockSpec((1, tq, LANES), lambda b, h, i: (b, i, h)),
        compiler_params=_cparams(("arbitrary", "arbitrary", "arbitrary")),
        name="attn_d",
    )(*lams, g_sub, q1, q2, k12, vd)


def _post_kernel(d_model, tiles_per_batch, om_ref, od_ref, x_ref, mod_ref, w_out_ref, g_ffn_ref, ws13_ref,
                 ws2_ref, wr_ref, br_ref,
                 xp_ref, h2_ref, eidx_ref, rank_ref, wts_ref, cnt_ref, carry_ref):
    i = pl.program_id(0)
    tm = x_ref.shape[0]
    n_exp = wr_ref.shape[0]
    per_group = n_exp // N_GROUPS

    @pl.when(i == 0)
    def _():
        carry_ref[...] = jnp.zeros_like(carry_ref)

    b = i // tiles_per_batch

    def modv(j):
        return mod_ref[pl.ds(b, 1), j * d_model:(j + 1) * d_model]

    half = om_ref.shape[1]
    y = _dot(om_ref[...], w_out_ref[0:half, :]) + _dot(od_ref[...], w_out_ref[half:, :])
    x1 = x_ref[...] + modv(2) * y
    h2 = _rms_rows(x1, g_ffn_ref[...]) * (1.0 + modv(4)) + modv(3)
    h2_ref[...] = h2

    h2b = h2.astype(BF16)
    gu = _dot(h2b, ws13_ref[...])
    ff = gu.shape[1] // 2
    g = gu[:, :ff]
    act = g * jax.nn.sigmoid(g) * gu[:, ff:]
    shared = _dot(act.astype(BF16), ws2_ref[...])
    xp_ref[...] = x1 + modv(5) * shared

    logits = lax.dot_general(wr_ref[...], h2, (((1,), (1,)), ((), ())), preferred_element_type=F32,
                             precision=lax.Precision.HIGHEST)
    s = jax.nn.sigmoid(logits)
    ssel = s + br_ref[...]
    s3 = ssel.reshape(N_GROUPS, per_group, tm)
    m1 = jnp.max(s3, axis=1, keepdims=True)
    eq = s3 == m1
    n_eq = jnp.sum(jnp.where(eq, 1.0, 0.0), axis=1, keepdims=True)
    m2 = jnp.max(jnp.where(eq, NEG_INF, s3), axis=1, keepdims=True)
    grp = m1 + jnp.where(n_eq >= 2.0, m1, m2)

    gi = lax.broadcasted_iota(jnp.int32, grp.shape, 0).astype(F32)
    gcur = grp
    gsel = jnp.zeros_like(grp)
    for _ in range(TOPK_GROUPS):
        gm = jnp.max(gcur, axis=0, keepdims=True)
        first = jnp.min(jnp.where(gcur == gm, gi, float(N_GROUPS)), axis=0, keepdims=True)
        oh = gi == first
        gsel = jnp.where(oh, 1.0, gsel)
        gcur = jnp.where(oh, NEG_INF, gcur)
    cur = jnp.where(gsel > 0.0, s3, NEG_INF).reshape(n_exp, tm)

    ie = lax.broadcasted_iota(jnp.int32, (n_exp, tm), 0).astype(F32)
    sel = jnp.zeros((n_exp, tm), F32)
    e_rows = []
    s_rows = []
    for _ in range(TOP_K):
        m = jnp.max(cur, axis=0, keepdims=True)
        first = jnp.min(jnp.where(cur == m, ie, float(n_exp)), axis=0, keepdims=True)
        oh = ie == first
        cur = jnp.where(oh, NEG_INF, cur)
        sel = jnp.where(oh, 1.0, sel)
        e_rows.append(first)
        s_rows.append(jnp.sum(jnp.where(oh, s, 0.0), axis=0, keepdims=True))
    s_tot = s_rows[0]
    for r in s_rows[1:]:
        s_tot = s_tot + r

    tr = lax.broadcasted_iota(jnp.int32, (tm, tm), 0)
    tc = lax.broadcasted_iota(jnp.int32, (tm, tm), 1)
    upper = jnp.where(tr < tc, 1.0, 0.0).astype(BF16)
    rank = _dot(sel.astype(BF16), upper) + carry_ref[...]
    for k in range(TOP_K):
        eidx_ref[k:k + 1, :] = e_rows[k].astype(jnp.int32)
        rank_ref[k:k + 1, :] = jnp.sum(jnp.where(ie == e_rows[k], rank, 0.0), axis=0,
                                       keepdims=True).astype(jnp.int32)
        wts_ref[k:k + 1, :] = s_rows[k] / s_tot * ROUTED_SCALE
    carry_ref[...] = carry_ref[...] + jnp.sum(sel, axis=1, keepdims=True)
    cnt_ref[...] = carry_ref[...].astype(jnp.int32)


def _post(om, od, x2d, mod, w_out, g_ffn, ws13, ws2, wr_t, br, n_lat, tm):
    t, d = x2d.shape
    n_exp = wr_t.shape[0]

    def full(a):
        return pl.BlockSpec(a.shape, lambda i: (0,) * a.ndim)

    row = lambda w: pl.BlockSpec((tm, w), lambda i: (i, 0))
    col = pl.BlockSpec((TOP_K, tm), lambda i: (0, i))
    return pl.pallas_call(
        functools.partial(_post_kernel, d, n_lat // tm),
        out_shape=(jax.ShapeDtypeStruct((t, d), F32), jax.ShapeDtypeStruct((t, d), F32),
                   jax.ShapeDtypeStruct((TOP_K, t), jnp.int32), jax.ShapeDtypeStruct((TOP_K, t), jnp.int32),
                   jax.ShapeDtypeStruct((TOP_K, t), F32), jax.ShapeDtypeStruct((n_exp, 1), jnp.int32)),
        grid=(t // tm,),
        in_specs=[row(om.shape[1]), row(od.shape[1]), row(d), full(mod), full(w_out), full(g_ffn),
                  full(ws13), full(ws2), full(wr_t), full(br)],
        out_specs=(row(d), row(d), col, col, col, pl.BlockSpec((n_exp, 1), lambda i: (0, 0))),
        scratch_shapes=[pltpu.VMEM((n_exp, 1), F32)],
        compiler_params=_cparams(("arbitrary",)),
        name="post",
    )(om, od, x2d, mod, w_out, g_ffn, ws13, ws2, wr_t, br)


def _row_copy(src, src_row, dst, dst_row, sem):
    return pltpu.make_async_copy(src.at[pl.ds(src_row, 1), :], dst.at[pl.ds(dst_row, 1), :], sem)


def _load_routing(i, td, eidx_hbm, rank_hbm, e_s, r_s, sem):
    ce = pltpu.make_async_copy(eidx_hbm.at[:, pl.ds(i * td, td)], e_s, sem.at[0])
    cr = pltpu.make_async_copy(rank_hbm.at[:, pl.ds(i * td, td)], r_s, sem.at[1])
    ce.start()
    cr.start()
    ce.wait()
    cr.wait()


def _dispatch_kernel(start_ref, h2_ref, eidx_hbm, rank_hbm, xs_hbm, e_s, r_s, sem):
    i = pl.program_id(0)
    td = h2_ref.shape[0]
    _load_routing(i, td, eidx_hbm, rank_hbm, e_s, r_s, sem)

    def start(j, c):
        for k in range(TOP_K):
            dst = start_ref[e_s[k, j]] + r_s[k, j]
            _row_copy(h2_ref, j, xs_hbm, dst, sem.at[2]).start()
        return c

    def wait(j, c):
        for k in range(TOP_K):
            _row_copy(h2_ref, 0, xs_hbm, 0, sem.at[2]).wait()
        return c

    lax.fori_loop(0, td, start, 0)
    lax.fori_loop(0, td, wait, 0)


def _dispatch(start, h2, eidx, rank, td):
    t, d = h2.shape
    any_spec = pl.BlockSpec(memory_space=pl.ANY)
    return pl.pallas_call(
        _dispatch_kernel,
        out_shape=jax.ShapeDtypeStruct((t * TOP_K, d), F32),
        grid_spec=pltpu.PrefetchScalarGridSpec(
            num_scalar_prefetch=1,
            grid=(t // td,),
            in_specs=[pl.BlockSpec((td, d), lambda i, *_: (i, 0)), any_spec, any_spec],
            out_specs=any_spec,
            scratch_shapes=[pltpu.SMEM((TOP_K, td), jnp.int32), pltpu.SMEM((TOP_K, td), jnp.int32),
                            pltpu.SemaphoreType.DMA((3,))]),
        compiler_params=_cparams(("arbitrary",)),
        name="dispatch",
    )(start, h2, eidx, rank)


def _experts_kernel(blk_ref, e_ref, lo_ref, hi_ref, xs_ref, w1_ref, w3_ref, w2_ref, ys_ref):
    j = pl.program_id(0)
    lo = lo_ref[j]
    hi = hi_ref[j]
    bm = xs_ref.shape[0]
    base = blk_ref[j] * bm

    def swiglu():
        xb = xs_ref[...].astype(BF16)
        g = _dot(xb, w1_ref[0].astype(BF16))
        u = _dot(xb, w3_ref[0].astype(BF16))
        act = g * jax.nn.sigmoid(g) * u
        return _dot(act.astype(BF16), w2_ref[0].astype(BF16))

    @pl.when(jnp.logical_and(hi > lo, lo == base))
    def _():
        ys_ref[...] = swiglu()

    @pl.when(jnp.logical_and(hi > lo, lo != base))
    def _():
        rows = base + lax.broadcasted_iota(jnp.int32, (bm, 1), 0)
        ys_ref[...] = jnp.where(rows >= lo, swiglu(), ys_ref[...])


def _experts(item_blk, item_e, item_lo, item_hi, xs, w1, w3, w2, bm):
    n_slots, d = xs.shape
    ff = w1.shape[2]
    rows = pl.BlockSpec((bm, d), lambda j, blk, e, lo, hi: (blk[j], 0))
    return pl.pallas_call(
        _experts_kernel,
        out_shape=jax.ShapeDtypeStruct((n_slots, d), F32),
        grid_spec=pltpu.PrefetchScalarGridSpec(
            num_scalar_prefetch=4,
            grid=(item_blk.shape[0],),
            in_specs=[rows,
                      pl.BlockSpec((1, d, ff), lambda j, blk, e, lo, hi: (e[j], 0, 0)),
                      pl.BlockSpec((1, d, ff), lambda j, blk, e, lo, hi: (e[j], 0, 0)),
                      pl.BlockSpec((1, ff, d), lambda j, blk, e, lo, hi: (e[j], 0, 0))],
            out_specs=rows),
        compiler_params=_cparams(("arbitrary",)),
        name="experts",
    )(item_blk, item_e, item_lo, item_hi, xs, w1, w3, w2)


def _combine_kernel(d_model, tiles_per_batch, pstart_ref, xp_ref, wts_ref, mod_ref, g_final_ref, eidx_hbm,
                    rank_hbm, ys_hbm, o_ref, e_s, r_s, gbuf, sem):
    i = pl.program_id(0)
    tc = xp_ref.shape[0]
    _load_routing(i, tc, eidx_hbm, rank_hbm, e_s, r_s, sem)

    def start(j, c):
        for k in range(TOP_K):
            src = pstart_ref[e_s[k, j]] + r_s[k, j]
            _row_copy(ys_hbm, src, gbuf.at[k], j, sem.at[2]).start()
        return c

    def wait(j, c):
        for k in range(TOP_K):
            _row_copy(ys_hbm, 0, gbuf.at[k], 0, sem.at[2]).wait()
        return c

    lax.fori_loop(0, tc, start, 0)
    lax.fori_loop(0, tc, wait, 0)

    w = wts_ref[...]
    routed = w[:, 0:1] * gbuf[0]
    for k in range(1, TOP_K):
        routed = routed + w[:, k:k + 1] * gbuf[k]
    b = i // tiles_per_batch
    gt2 = mod_ref[pl.ds(b, 1), 5 * d_model:6 * d_model]
    o_ref[...] = _rms_rows(xp_ref[...] + gt2 * routed, g_final_ref[...])


def _combine(pstart, xp, wts_t, mod, g_final, eidx, rank, ys, n_lat, tc):
    t, d = xp.shape
    any_spec = pl.BlockSpec(memory_space=pl.ANY)
    return pl.pallas_call(
        functools.partial(_combine_kernel, d, n_lat // tc),
        out_shape=jax.ShapeDtypeStruct((t, d), F32),
        grid_spec=pltpu.PrefetchScalarGridSpec(
            num_scalar_prefetch=1,
            grid=(t // tc,),
            in_specs=[pl.BlockSpec((tc, d), lambda i, *_: (i, 0)),
                      pl.BlockSpec((tc, TOP_K), lambda i, *_: (i, 0)),
                      pl.BlockSpec(mod.shape, lambda i, *_: (0, 0)),
                      pl.BlockSpec(g_final.shape, lambda i, *_: (0, 0)),
                      any_spec, any_spec, any_spec],
            out_specs=pl.BlockSpec((tc, d), lambda i, *_: (i, 0)),
            scratch_shapes=[pltpu.SMEM((TOP_K, tc), jnp.int32), pltpu.SMEM((TOP_K, tc), jnp.int32),
                            pltpu.VMEM((TOP_K, tc, d), F32), pltpu.SemaphoreType.DMA((3,))]),
        compiler_params=_cparams(("arbitrary",)),
        name="combine",
    )(pstart, xp, wts_t, mod, g_final, eidx, rank, ys)


def _swap_halves(w, width):
    r, c = w.shape
    return w.reshape(r, c // width, 2, width // 2)[:, :, ::-1, :].reshape(r, c)


def _pad_cols(w, left, total):
    return jnp.pad(w, ((0, 0), (left, total - left - w.shape[1])))


def _layout_weights(w_in, w_uq, w_ukv):
    s0 = MLA_Q_RANK
    s1 = s0 + MLA_KV_RANK
    s2 = s1 + MLA_ROPE
    s3 = s2 + DIFF_HEADS * 2 * DIFF_QK
    s4 = s3 + DIFF_HEADS * 2 * DIFF_QK
    w_kr, w_dq, w_dk = w_in[:, s1:s2], w_in[:, s2:s3], w_in[:, s3:s4]
    w_in_ext = jnp.concatenate([
        w_in[:, :s1], w_dq, _swap_halves(w_dq, DIFF_QK), w_dk, _swap_halves(w_dk, DIFF_QK), w_in[:, s4:],
        _pad_cols(w_kr, MLA_NOPE, LANES), _pad_cols(_swap_halves(w_kr, MLA_ROPE), MLA_NOPE, LANES)],
        axis=1).astype(BF16)

    qk = MLA_NOPE + MLA_ROPE
    qa, qb, kk, vv = [], [], [], []
    for h in range(MLA_HEADS):
        wq = w_uq[:, h * qk:(h + 1) * qk]
        qa.append(_pad_cols(wq, 0, LANES))
        qb.append(_pad_cols(_swap_halves(wq[:, MLA_NOPE:], MLA_ROPE), MLA_NOPE, LANES))
        wkv = w_ukv[:, h * (MLA_NOPE + MLA_V):(h + 1) * (MLA_NOPE + MLA_V)]
        kk.append(_pad_cols(wkv[:, :MLA_NOPE], 0, LANES))
        vv.append(_pad_cols(wkv[:, MLA_NOPE:], (h % 2) * MLA_V, LANES))
    w_uq_ext = jnp.concatenate(qa + qb, axis=1).astype(BF16)
    w_ukv_ext = jnp.concatenate(kk + vv, axis=1).astype(BF16)
    return w_in_ext, w_uq_ext, w_ukv_ext


def _rotary_tables(n_ctx, n_lat):
    def angles(rot_dim):
        n_freq = rot_dim // 4
        inv = ROPE_BASE ** (-(jnp.arange(n_freq, dtype=F32) / n_freq))
        rows = n_lat // GRID_W
        row = jnp.repeat(jnp.arange(rows, dtype=F32), GRID_W)
        col = jnp.tile(jnp.arange(GRID_W, dtype=F32), rows)
        theta = jnp.concatenate([row[:, None] * inv, col[:, None] * inv], axis=-1)
        theta = jnp.concatenate([jnp.zeros((n_ctx, 2 * n_freq), F32), theta], axis=0)
        return jnp.cos(theta), jnp.sin(theta)

    n = n_ctx + n_lat
    cm, sm = angles(MLA_ROPE)
    cd, sd = angles(DIFF_QK)
    zeros = lambda w: jnp.zeros((n, w), F32)
    ones = lambda w: jnp.ones((n, w), F32)
    pad_m = LANES - MLA_NOPE - MLA_ROPE
    scale_m = 1.0 / math.sqrt(MLA_NOPE + MLA_ROPE)
    scale_d = 1.0 / math.sqrt(DIFF_QK)
    cq = jnp.concatenate([ones(MLA_NOPE), cm, cm, zeros(pad_m)], axis=1)
    sq = jnp.concatenate([zeros(MLA_NOPE), -sm, sm, zeros(pad_m)], axis=1)
    ck = jnp.concatenate([zeros(MLA_NOPE), cm, cm, zeros(pad_m)], axis=1)
    cdd = jnp.concatenate([cd, cd, cd, cd], axis=1)
    sdd = jnp.concatenate([-sd, sd, -sd, sd], axis=1)
    lo = jnp.concatenate([ones(DIFF_QK), zeros(DIFF_QK)], axis=1)
    hi = 1.0 - lo
    return jnp.concatenate([cq * scale_m, sq * scale_m, ck, sq,
                            cdd * lo * scale_d, sdd * lo * scale_d, cdd * hi * scale_d, sdd * hi * scale_d,
                            cdd, sdd], axis=1)


def kernel(x, c, ctx, c_ctx, w_mod, b_mod, g_attn, g_ffn, w_in, g_q_lat, w_uq, g_kv_lat, w_ukv, lam_q1, lam_k1,
           lam_q2, lam_k2, g_subln, w_out, w_router, router_bias, w1, w3, w2, ws1, ws3, ws2, g_final):
    n_batch, n_lat, d = x.shape
    n_ctx = ctx.shape[1]
    t = n_batch * n_lat
    tm = 256
    tq = 512
    bm = 128
    td = 128
    assert w_mod.shape[0] == 1 and n_ctx % tm == 0 and n_lat % tq == 0 and n_lat % tm == 0 and t % td == 0
    assert (t * TOP_K) % bm == 0

    cc = jnp.concatenate([c, c_ctx[None, :], jnp.zeros((8 - n_batch - 1, d), F32)], axis=0)
    mod = _modulation(cc, w_mod[0], b_mod)

    w_in_ext, w_uq_ext, w_ukv_ext = _layout_weights(w_in[0], w_uq[0], w_ukv[0])
    tables = _rotary_tables(n_ctx, n_lat)
    qm, km, vm, q1, q2, k12, vd = _projections(ctx, x, mod, g_attn, w_in_ext, g_q_lat, w_uq_ext, g_kv_lat,
                                               w_ukv_ext, tables, tm)
    om = _attn_mla(qm, km, vm, tq)
    od = _attn_diff((lam_q1, lam_k1, lam_q2, lam_k2), g_subln, q1, q2, k12, vd, tq)

    ws13 = jnp.concatenate([ws1[0], ws3[0]], axis=1).astype(BF16)
    xp, h2, eidx, rank, wts, cnt = _post(
        om.reshape(t, -1), od.reshape(t, -1), x.reshape(t, d), mod, w_out[0].astype(BF16), g_ffn, ws13,
        ws2[0].astype(BF16), w_router[0].T, router_bias[0][:, None], n_lat, tm)

    counts = cnt[:, 0]
    start = (jnp.cumsum(counts) - counts).astype(jnp.int32)
    n_slots = t * TOP_K
    n_blocks = n_slots // bm
    bounds = jnp.sort(jnp.concatenate([jnp.arange(n_blocks, dtype=jnp.int32) * bm, start[1:],
                                       jnp.full((1,), n_slots, jnp.int32)]))
    item_lo, item_hi = bounds[:-1], bounds[1:]
    item_blk = jnp.minimum(item_lo // bm, n_blocks - 1)
    item_e = jnp.clip(jnp.searchsorted(start, item_lo, side="right") - 1, 0, N_EXPERTS - 1).astype(jnp.int32)

    xs = _dispatch(start, h2, eidx, rank, td)
    ys = _experts(item_blk, item_e, item_lo, item_hi, xs, w1[0], w3[0], w2[0], bm)
    out = _combine(start, xp, wts.T, mod, g_final[None, :], eidx, rank, ys, n_lat, td)
    return out.reshape(n_batch, n_lat, d)
```

```python
import functools
import math

import jax
import jax.numpy as jnp
from jax import lax
from jax.experimental import pallas as pl
from jax.experimental.pallas import tpu as pltpu

GRID_W = 64
ROPE_BASE = 10000.0
NORM_EPS = 1e-6
MLA_HEADS = 8
MLA_NOPE = 64
MLA_ROPE = 32
MLA_V = 64
MLA_Q_RANK = 256
MLA_KV_RANK = 128
DIFF_HEADS = 4
DIFF_QK = 64
DIFF_V = 2 * DIFF_QK
N_EXPERTS = 256
TOP_K = 8
N_GROUPS = 8
TOPK_GROUPS = 4
ROUTED_SCALE = 2.5
LAM_INIT = 0.8 - 0.6 * math.exp(-0.3 * 0)

LANES = 128
VMEM_LIMIT = 48 * 1024 * 1024

F32 = jnp.float32
BF16 = jnp.bfloat16
NEG_INF = float("-inf")


def _cparams(sem):
    return pltpu.CompilerParams(dimension_semantics=sem, vmem_limit_bytes=VMEM_LIMIT)


def _rms_rows(x, g):
    return x * lax.rsqrt(jnp.mean(x * x, axis=-1, keepdims=True) + NORM_EPS) * g


def _dot(a, b):
    return jnp.dot(a, b, preferred_element_type=F32)


def _dot_nt(a, b):
    return lax.dot_general(a, b, (((1,), (1,)), ((), ())), preferred_element_type=F32)


def _mod_kernel(c_ref, w_ref, b_ref, o_ref):
    a = c_ref[...]
    a = a * jax.nn.sigmoid(a)
    o_ref[...] = jnp.dot(a, w_ref[...], preferred_element_type=F32,
                         precision=lax.Precision.HIGHEST) + b_ref[...]


def _modulation(cc, w_mod, b_mod):
    rows, d = cc.shape
    cols = w_mod.shape[1]
    tn = 1536
    return pl.pallas_call(
        _mod_kernel,
        out_shape=jax.ShapeDtypeStruct((rows, cols), F32),
        grid=(cols // tn,),
        in_specs=[pl.BlockSpec((rows, d), lambda j: (0, 0)),
                  pl.BlockSpec((d, tn), lambda j: (0, j)),
                  pl.BlockSpec((1, tn), lambda j: (0, j))],
        out_specs=pl.BlockSpec((rows, tn), lambda j: (0, j)),
        compiler_params=_cparams(("arbitrary",)),
        name="mod",
    )(cc, w_mod, b_mod)


_C_Q = 0
_C_KV = _C_Q + MLA_Q_RANK
_C_DQ = _C_KV + MLA_KV_RANK
_C_DQS = _C_DQ + DIFF_HEADS * 2 * DIFF_QK
_C_DK = _C_DQS + DIFF_HEADS * 2 * DIFF_QK
_C_DKS = _C_DK + DIFF_HEADS * 2 * DIFF_QK
_C_DV = _C_DKS + DIFF_HEADS * 2 * DIFF_QK
_C_KRA = _C_DV + DIFF_HEADS * DIFF_V
_C_KRB = _C_KRA + LANES
_IN_EXT = _C_KRB + LANES
_N_TAB = 10


def _proj_kernel(n_batch, d_model, ctx_ref, x_ref, mod_ref, g_attn_ref, w_in_ref, g_q_ref, w_uq_ref,
                 g_kv_ref, w_ukv_ref, tab_ref,
                 qm_ref, km_ref, vmt_ref, qd_ref, k12_ref, vdt_ref, *, n_ctx_tiles):
    b = pl.program_id(0)
    i = pl.program_id(1)
    is_ctx = i < n_ctx_tiles
    xin = jnp.where(is_ctx, ctx_ref[0], x_ref[0])
    row = jnp.where(is_ctx, n_batch, b)
    sh1 = mod_ref[pl.ds(row, 1), 0:d_model]
    sc1 = mod_ref[pl.ds(row, 1), d_model:2 * d_model]
    h = _rms_rows(xin, g_attn_ref[...]) * (1.0 + sc1) + sh1
    p = _dot(h.astype(BF16), w_in_ref[...])

    cq = _rms_rows(p[:, _C_Q:_C_Q + MLA_Q_RANK], g_q_ref[...])
    qa = _dot(cq.astype(BF16), w_uq_ref[...])
    ckv = _rms_rows(p[:, _C_KV:_C_KV + MLA_KV_RANK], g_kv_ref[...])
    kv = _dot(ckv.astype(BF16), w_ukv_ref[...])

    def tab(j):
        return tab_ref[:, j * LANES:(j + 1) * LANES]

    kr = p[:, _C_KRA:_C_KRA + LANES] * tab(2) + p[:, _C_KRB:_C_KRB + LANES] * tab(3)
    hw = MLA_HEADS * LANES
    for hd in range(MLA_HEADS):
        lo = hd * LANES
        qm_ref[0, hd] = (qa[:, lo:lo + LANES] * tab(0) + qa[:, hw + lo:hw + lo + LANES] * tab(1)).astype(BF16)
        km_ref[0, hd] = (kv[:, lo:lo + LANES] + kr).astype(BF16)
    for pr in range(MLA_HEADS // 2):
        lo = hw + pr * LANES
        vmt_ref[0, pr] = kv[:, lo:lo + LANES].T.astype(BF16)
    for hd in range(DIFF_HEADS):
        lo = hd * LANES
        dq = p[:, _C_DQ + lo:_C_DQ + lo + LANES]
        dqs = p[:, _C_DQS + lo:_C_DQS + lo + LANES]
        qd_ref[0, hd, 0] = (dq * tab(4) + dqs * tab(5)).astype(BF16)
        qd_ref[0, hd, 1] = (dq * tab(6) + dqs * tab(7)).astype(BF16)
        dk = p[:, _C_DK + lo:_C_DK + lo + LANES]
        dks = p[:, _C_DKS + lo:_C_DKS + lo + LANES]
        k12_ref[0, hd] = (dk * tab(8) + dks * tab(9)).astype(BF16)
        vdt_ref[0, hd] = p[:, _C_DV + lo:_C_DV + lo + LANES].T.astype(BF16)


def _projections(ctx, x, mod, g_attn, w_in_ext, g_q, w_uq_ext, g_kv, w_ukv_ext, tables, tm):
    n_batch, n_ctx, d = ctx.shape
    n_lat = x.shape[1]
    n_all = n_ctx + n_lat
    nct = n_ctx // tm
    nt = n_all // tm

    def full(a):
        return pl.BlockSpec(a.shape, lambda b, i: (0,) * a.ndim)

    def k_spec(heads):
        return pl.BlockSpec((1, heads, tm, LANES), lambda b, i: (b, 0, i, 0))

    def vt_spec(heads):
        return pl.BlockSpec((1, heads, LANES, tm), lambda b, i: (b, 0, 0, i))

    def lat(i):
        return jnp.maximum(i - nct, 0)

    def sds(*shape):
        return jax.ShapeDtypeStruct((n_batch,) + shape, BF16)

    return pl.pallas_call(
        functools.partial(_proj_kernel, n_batch, d, n_ctx_tiles=nct),
        out_shape=(sds(MLA_HEADS, n_lat, LANES), sds(MLA_HEADS, n_all, LANES), sds(MLA_HEADS // 2, LANES, n_all),
                   sds(DIFF_HEADS, 2, n_lat, LANES), sds(DIFF_HEADS, n_all, LANES),
                   sds(DIFF_HEADS, LANES, n_all)),
        grid=(n_batch, nt),
        in_specs=[pl.BlockSpec((1, tm, d), lambda b, i: (b, jnp.minimum(i, nct - 1), 0)),
                  pl.BlockSpec((1, tm, d), lambda b, i: (b, jnp.maximum(i - nct, 0), 0)),
                  full(mod), full(g_attn), full(w_in_ext), full(g_q), full(w_uq_ext), full(g_kv),
                  full(w_ukv_ext),
                  pl.BlockSpec((tm, _N_TAB * LANES), lambda b, i: (i, 0))],
        out_specs=(pl.BlockSpec((1, MLA_HEADS, tm, LANES), lambda b, i: (b, 0, lat(i), 0)),
                   k_spec(MLA_HEADS), vt_spec(MLA_HEADS // 2),
                   pl.BlockSpec((1, DIFF_HEADS, 2, tm, LANES), lambda b, i: (b, 0, 0, lat(i), 0)),
                   k_spec(DIFF_HEADS), vt_spec(DIFF_HEADS)),
        compiler_params=_cparams(("arbitrary", "arbitrary")),
        name="proj",
    )(ctx, x, mod, g_attn, w_in_ext, g_q, w_uq_ext, g_kv, w_ukv_ext, tables)


def _attend_t(q, k, vt):
    st = _dot_nt(k, q)
    et = jnp.exp2(st - jnp.max(st, axis=0, keepdims=True))
    l = jnp.sum(et, axis=0, keepdims=True)
    return _dot(vt, et.astype(BF16)) * (1.0 / l)


def _attn_mla_kernel(q_ref, k_ref, vt_ref, o_ref):
    outs = [_attend_t(q_ref[0, j], k_ref[0, j], vt_ref[0, 0, j * MLA_V:(j + 1) * MLA_V, :]) for j in range(2)]
    o_ref[0] = jnp.concatenate(outs, axis=0).T.astype(BF16)


def _attn_mla(qm, km, vmt, tq):
    n_batch, heads, n_lat, _ = qm.shape
    n_all = km.shape[2]
    return pl.pallas_call(
        _attn_mla_kernel,
        out_shape=jax.ShapeDtypeStruct((n_batch, n_lat, heads // 2 * LANES), BF16),
        grid=(n_batch, heads // 2, n_lat // tq),
        in_specs=[pl.BlockSpec((1, 2, tq, LANES), lambda b, h, i: (b, h, i, 0)),
                  pl.BlockSpec((1, 2, n_all, LANES), lambda b, h, i: (b, h, 0, 0)),
                  pl.BlockSpec((1, 1, LANES, n_all), lambda b, h, i: (b, h, 0, 0))],
        out_specs=pl.BlockSpec((1, tq, LANES), lambda b, h, i: (b, i, h)),
        compiler_params=_cparams(("arbitrary", "arbitrary", "arbitrary")),
        name="attn_m",
    )(qm, km, vmt)


def _attn_diff_kernel(lq1_ref, lk1_ref, lq2_ref, lk2_ref, g_sub_ref, q_ref, k_ref, vt_ref, o_ref):
    lam = (jnp.exp(jnp.sum(lq1_ref[...] * lk1_ref[...], axis=-1, keepdims=True))
           - jnp.exp(jnp.sum(lq2_ref[...] * lk2_ref[...], axis=-1, keepdims=True)) + LAM_INIT)
    tq = q_ref.shape[3]
    o = _attend_t(q_ref[0, 0].reshape(2 * tq, LANES), k_ref[0, 0], vt_ref[0, 0])
    ot = o[:, :tq] - lam * o[:, tq:]
    ot = ot * lax.rsqrt(jnp.mean(ot * ot, axis=0, keepdims=True) + NORM_EPS) * g_sub_ref[...]
    o_ref[0] = (ot * (1.0 - LAM_INIT)).T.astype(BF16)


def _attn_diff(lams, g_sub_col, qd, k12, vdt, tq):
    n_batch, heads, _, n_lat, _ = qd.shape
    n_all = k12.shape[2]

    def small(a):
        return pl.BlockSpec(a.shape, lambda b, h, i: (0,) * a.ndim)

    return pl.pallas_call(
        _attn_diff_kernel,
        out_shape=jax.ShapeDtypeStruct((n_batch, n_lat, heads * LANES), BF16),
        grid=(n_batch, heads, n_lat // tq),
        in_specs=[small(lams[0]), small(lams[1]), small(lams[2]), small(lams[3]), small(g_sub_col),
                  pl.BlockSpec((1, 1, 2, tq, LANES), lambda b, h, i: (b, h, 0, i, 0)),
                  pl.BlockSpec((1, 1, n_all, LANES), lambda b, h, i: (b, h, 0, 0)),
                  pl.BlockSpec((1, 1, LANES, n_all), lambda b, h, i: (b, h, 0, 0))],
        out_specs=pl.BlockSpec((1, tq, LANES), lambda b, h, i: (b, i, h)),
        compiler_params=_cparams(("arbitrary", "arbitrary", "arbitrary")),
        name="attn_d",
    )(*lams, g_sub_col, qd, k12, vdt)


def _post_kernel(d_model, tiles_per_batch, om_ref, od_ref, x_ref, mod_ref, w_out_ref, g_ffn_ref, ws13_ref,
                 ws2_ref, wr_ref, br_ref,
                 xp_ref, h2_ref, eidx_ref, rank_ref, wts_ref, cnt_ref, carry_ref):
    i = pl.program_id(0)
    tm = x_ref.shape[0]
    n_exp = wr_ref.shape[0]
    per_group = n_exp // N_GROUPS

    @pl.when(i == 0)
    def _():
        carry_ref[...] = jnp.zeros_like(carry_ref)

    b = i // tiles_per_batch

    def modv(j):
        return mod_ref[pl.ds(b, 1), j * d_model:(j + 1) * d_model]

    half = om_ref.shape[1]
    y = _dot(om_ref[...], w_out_ref[0:half, :]) + _dot(od_ref[...], w_out_ref[half:, :])
    x1 = x_ref[...] + modv(2) * y
    h2 = _rms_rows(x1, g_ffn_ref[...]) * (1.0 + modv(4)) + modv(3)
    h2_ref[...] = h2

    h2b = h2.astype(BF16)
    gu = _dot(h2b, ws13_ref[...])
    ff = gu.shape[1] // 2
    g = gu[:, :ff]
    act = g * jax.nn.sigmoid(g) * gu[:, ff:]
    shared = _dot(act.astype(BF16), ws2_ref[...])
    xp_ref[...] = x1 + modv(5) * shared

    logits = lax.dot_general(wr_ref[...], h2, (((1,), (1,)), ((), ())), preferred_element_type=F32,
                             precision=lax.Precision.HIGHEST)
    s = jax.nn.sigmoid(logits)
    ssel = s + br_ref[...]
    s3 = ssel.reshape(N_GROUPS, per_group, tm)
    m1 = jnp.max(s3, axis=1, keepdims=True)
    eq = s3 == m1
    n_eq = jnp.sum(jnp.where(eq, 1.0, 0.0), axis=1, keepdims=True)
    m2 = jnp.max(jnp.where(eq, NEG_INF, s3), axis=1, keepdims=True)
    grp = m1 + jnp.where(n_eq >= 2.0, m1, m2)

    gi = lax.broadcasted_iota(jnp.int32, grp.shape, 0).astype(F32)
    gcur = grp
    gsel = jnp.zeros_like(grp)
    for _ in range(TOPK_GROUPS):
        gm = jnp.max(gcur, axis=0, keepdims=True)
        first = jnp.min(jnp.where(gcur == gm, gi, float(N_GROUPS)), axis=0, keepdims=True)
        oh = gi == first
        gsel = jnp.where(oh, 1.0, gsel)
        gcur = jnp.where(oh, NEG_INF, gcur)
    cur = jnp.where(gsel > 0.0, s3, NEG_INF).reshape(n_exp, tm)

    ie = lax.broadcasted_iota(jnp.int32, (n_exp, tm), 0).astype(F32)
    sel = jnp.zeros((n_exp, tm), F32)
    e_rows = []
    s_rows = []
    for _ in range(TOP_K):
        m = jnp.max(cur, axis=0, keepdims=True)
        first = jnp.min(jnp.where(cur == m, ie, float(n_exp)), axis=0, keepdims=True)
        oh = ie == first
        cur = jnp.where(oh, NEG_INF, cur)
        sel = jnp.where(oh, 1.0, sel)
        e_rows.append(first)
        s_rows.append(jnp.sum(jnp.where(oh, s, 0.0), axis=0, keepdims=True))
    s_tot = s_rows[0]
    for r in s_rows[1:]:
        s_tot = s_tot + r

    tr = lax.broadcasted_iota(jnp.int32, (tm, tm), 0)
    tc = lax.broadcasted_iota(jnp.int32, (tm, tm), 1)
    upper = jnp.where(tr < tc, 1.0, 0.0).astype(BF16)
    rank = _dot(sel.astype(BF16), upper) + carry_ref[...]
    for k in range(TOP_K):
        eidx_ref[k:k + 1, :] = e_rows[k].astype(jnp.int32)
        rank_ref[k:k + 1, :] = jnp.sum(jnp.where(ie == e_rows[k], rank, 0.0), axis=0,
                                       keepdims=True).astype(jnp.int32)
        wts_ref[k:k + 1, :] = s_rows[k] / s_tot * ROUTED_SCALE
    carry_ref[...] = carry_ref[...] + jnp.sum(sel, axis=1, keepdims=True)
    cnt_ref[...] = carry_ref[...].astype(jnp.int32)


def _post(om, od, x2d, mod, w_out, g_ffn, ws13, ws2, wr_t, br, n_lat, tm):
    t, d = x2d.shape
    n_exp = wr_t.shape[0]

    def full(a):
        return pl.BlockSpec(a.shape, lambda i: (0,) * a.ndim)

    row = lambda w: pl.BlockSpec((tm, w), lambda i: (i, 0))
    col = pl.BlockSpec((TOP_K, tm), lambda i: (0, i))
    return pl.pallas_call(
        functools.partial(_post_kernel, d, n_lat // tm),
        out_shape=(jax.ShapeDtypeStruct((t, d), F32), jax.ShapeDtypeStruct((t, d), F32),
                   jax.ShapeDtypeStruct((TOP_K, t), jnp.int32), jax.ShapeDtypeStruct((TOP_K, t), jnp.int32),
                   jax.ShapeDtypeStruct((TOP_K, t), F32), jax.ShapeDtypeStruct((n_exp, 1), jnp.int32)),
        grid=(t // tm,),
        in_specs=[row(om.shape[1]), row(od.shape[1]), row(d), full(mod), full(w_out), full(g_ffn),
                  full(ws13), full(ws2), full(wr_t), full(br)],
        out_specs=(row(d), row(d), col, col, col, pl.BlockSpec((n_exp, 1), lambda i: (0, 0))),
        scratch_shapes=[pltpu.VMEM((n_exp, 1), F32)],
        compiler_params=_cparams(("arbitrary",)),
        name="post",
    )(om, od, x2d, mod, w_out, g_ffn, ws13, ws2, wr_t, br)


def _row_copy(src, src_row, dst, dst_row, sem):
    return pltpu.make_async_copy(src.at[pl.ds(src_row, 1), :], dst.at[pl.ds(dst_row, 1), :], sem)


def _load_routing(i, td, eidx_hbm, rank_hbm, e_s, r_s, sem):
    ce = pltpu.make_async_copy(eidx_hbm.at[:, pl.ds(i * td, td)], e_s, sem.at[0])
    cr = pltpu.make_async_copy(rank_hbm.at[:, pl.ds(i * td, td)], r_s, sem.at[1])
    ce.start()
    cr.start()
    ce.wait()
    cr.wait()


def _dispatch_kernel(start_ref, h2_ref, eidx_hbm, rank_hbm, xs_hbm, e_s, r_s, sem):
    i = pl.program_id(0)
    td = h2_ref.shape[0]
    _load_routing(i, td, eidx_hbm, rank_hbm, e_s, r_s, sem)

    def start(j, c):
        for k in range(TOP_K):
            dst = start_ref[e_s[k, j]] + r_s[k, j]
            _row_copy(h2_ref, j, xs_hbm, dst, sem.at[2]).start()
        return c

    def wait(j, c):
        for k in range(TOP_K):
            _row_copy(h2_ref, 0, xs_hbm, 0, sem.at[2]).wait()
        return c

    lax.fori_loop(0, td, start, 0)
    lax.fori_loop(0, td, wait, 0)


def _dispatch(start, h2, eidx, rank, td):
    t, d = h2.shape
    any_spec = pl.BlockSpec(memory_space=pl.ANY)
    return pl.pallas_call(
        _dispatch_kernel,
        out_shape=jax.ShapeDtypeStruct((t * TOP_K, d), F32),
        grid_spec=pltpu.PrefetchScalarGridSpec(
            num_scalar_prefetch=1,
            grid=(t // td,),
            in_specs=[pl.BlockSpec((td, d), lambda i, *_: (i, 0)), any_spec, any_spec],
            out_specs=any_spec,
            scratch_shapes=[pltpu.SMEM((TOP_K, td), jnp.int32), pltpu.SMEM((TOP_K, td), jnp.int32),
                            pltpu.SemaphoreType.DMA((3,))]),
        compiler_params=_cparams(("arbitrary",)),
        name="dispatch",
    )(start, h2, eidx, rank)


def _experts_kernel(blk_ref, e_ref, lo_ref, hi_ref, xs_ref, w1_ref, w3_ref, w2_ref, ys_ref, w13_b, w2_b):
    j = pl.program_id(0)
    lo = lo_ref[j]
    hi = hi_ref[j]
    bm = xs_ref.shape[0]
    ff = w1_ref.shape[2]
    base = blk_ref[j] * bm

    @pl.when(jnp.logical_or(j == 0, e_ref[j] != e_ref[jnp.maximum(j - 1, 0)]))
    def _():
        w13_b[:, :ff] = w1_ref[0].astype(BF16)
        w13_b[:, ff:] = w3_ref[0].astype(BF16)
        w2_b[...] = w2_ref[0].astype(BF16)

    def swiglu():
        gu = _dot(xs_ref[...].astype(BF16), w13_b[...])
        g = gu[:, :ff]
        act = g * jax.nn.sigmoid(g) * gu[:, ff:]
        return _dot(act.astype(BF16), w2_b[...])

    @pl.when(jnp.logical_and(hi > lo, lo == base))
    def _():
        ys_ref[...] = swiglu()

    @pl.when(jnp.logical_and(hi > lo, lo != base))
    def _():
        rows = base + lax.broadcasted_iota(jnp.int32, (bm, 1), 0)
        ys_ref[...] = jnp.where(rows >= lo, swiglu(), ys_ref[...])


def _experts(item_blk, item_e, item_lo, item_hi, xs, w1, w3, w2, bm):
    n_slots, d = xs.shape
    ff = w1.shape[2]
    rows = pl.BlockSpec((bm, d), lambda j, blk, e, lo, hi: (blk[j], 0))
    return pl.pallas_call(
        _experts_kernel,
        out_shape=jax.ShapeDtypeStruct((n_slots, d), F32),
        grid_spec=pltpu.PrefetchScalarGridSpec(
            num_scalar_prefetch=4,
            grid=(item_blk.shape[0],),
            in_specs=[rows,
                      pl.BlockSpec((1, d, ff), lambda j, blk, e, lo, hi: (e[j], 0, 0)),
                      pl.BlockSpec((1, d, ff), lambda j, blk, e, lo, hi: (e[j], 0, 0)),
                      pl.BlockSpec((1, ff, d), lambda j, blk, e, lo, hi: (e[j], 0, 0))],
            out_specs=rows,
            scratch_shapes=[pltpu.VMEM((d, 2 * ff), BF16), pltpu.VMEM((ff, d), BF16)]),
        compiler_params=_cparams(("arbitrary",)),
        name="experts",
    )(item_blk, item_e, item_lo, item_hi, xs, w1, w3, w2)


def _combine_kernel(d_model, tiles_per_batch, pstart_ref, xp_ref, wts_ref, mod_ref, g_final_ref, eidx_hbm,
                    rank_hbm, ys_hbm, o_ref, e_s, r_s, gbuf, sem):
    i = pl.program_id(0)
    tc = xp_ref.shape[0]
    _load_routing(i, tc, eidx_hbm, rank_hbm, e_s, r_s, sem)

    def start(j, c):
        for k in range(TOP_K):
            src = pstart_ref[e_s[k, j]] + r_s[k, j]
            _row_copy(ys_hbm, src, gbuf.at[k], j, sem.at[2]).start()
        return c

    def wait(j, c):
        for k in range(TOP_K):
            _row_copy(ys_hbm, 0, gbuf.at[k], 0, sem.at[2]).wait()
        return c

    lax.fori_loop(0, tc, start, 0)
    lax.fori_loop(0, tc, wait, 0)

    w = wts_ref[...]
    routed = w[:, 0:1] * gbuf[0]
    for k in range(1, TOP_K):
        routed = routed + w[:, k:k + 1] * gbuf[k]
    b = i // tiles_per_batch
    gt2 = mod_ref[pl.ds(b, 1), 5 * d_model:6 * d_model]
    o_ref[...] = _rms_rows(xp_ref[...] + gt2 * routed, g_final_ref[...])


def _combine(pstart, xp, wts_t, mod, g_final, eidx, rank, ys, n_lat, tc):
    t, d = xp.shape
    any_spec = pl.BlockSpec(memory_space=pl.ANY)
    return pl.pallas_call(
        functools.partial(_combine_kernel, d, n_lat // tc),
        out_shape=jax.ShapeDtypeStruct((t, d), F32),
        grid_spec=pltpu.PrefetchScalarGridSpec(
            num_scalar_prefetch=1,
            grid=(t // tc,),
            in_specs=[pl.BlockSpec((tc, d), lambda i, *_: (i, 0)),
                      pl.BlockSpec((tc, TOP_K), lambda i, *_: (i, 0)),
                      pl.BlockSpec(mod.shape, lambda i, *_: (0, 0)),
                      pl.BlockSpec(g_final.shape, lambda i, *_: (0, 0)),
                      any_spec, any_spec, any_spec],
            out_specs=pl.BlockSpec((tc, d), lambda i, *_: (i, 0)),
            scratch_shapes=[pltpu.SMEM((TOP_K, tc), jnp.int32), pltpu.SMEM((TOP_K, tc), jnp.int32),
                            pltpu.VMEM((TOP_K, tc, d), F32), pltpu.SemaphoreType.DMA((3,))]),
        compiler_params=_cparams(("arbitrary",)),
        name="combine",
    )(pstart, xp, wts_t, mod, g_final, eidx, rank, ys)


def _swap_halves(w, width):
    r, c = w.shape
    return w.reshape(r, c // width, 2, width // 2)[:, :, ::-1, :].reshape(r, c)


def _pad_cols(w, left, total):
    return jnp.pad(w, ((0, 0), (left, total - left - w.shape[1])))


def _layout_weights(w_in, w_uq, w_ukv):
    s0 = MLA_Q_RANK
    s1 = s0 + MLA_KV_RANK
    s2 = s1 + MLA_ROPE
    s3 = s2 + DIFF_HEADS * 2 * DIFF_QK
    s4 = s3 + DIFF_HEADS * 2 * DIFF_QK
    w_kr, w_dq, w_dk = w_in[:, s1:s2], w_in[:, s2:s3], w_in[:, s3:s4]
    w_in_ext = jnp.concatenate([
        w_in[:, :s1], w_dq, _swap_halves(w_dq, DIFF_QK), w_dk, _swap_halves(w_dk, DIFF_QK), w_in[:, s4:],
        _pad_cols(w_kr, MLA_NOPE, LANES), _pad_cols(_swap_halves(w_kr, MLA_ROPE), MLA_NOPE, LANES)],
        axis=1).astype(BF16)

    qk = MLA_NOPE + MLA_ROPE
    qa, qb, kk, vv = [], [], [], []
    for h in range(MLA_HEADS):
        wq = w_uq[:, h * qk:(h + 1) * qk]
        qa.append(_pad_cols(wq, 0, LANES))
        qb.append(_pad_cols(_swap_halves(wq[:, MLA_NOPE:], MLA_ROPE), MLA_NOPE, LANES))
        wkv = w_ukv[:, h * (MLA_NOPE + MLA_V):(h + 1) * (MLA_NOPE + MLA_V)]
        kk.append(_pad_cols(wkv[:, :MLA_NOPE], 0, LANES))
        vv.append(wkv[:, MLA_NOPE:])
    w_uq_ext = jnp.concatenate(qa + qb, axis=1).astype(BF16)
    w_ukv_ext = jnp.concatenate(kk + vv, axis=1).astype(BF16)
    return w_in_ext, w_uq_ext, w_ukv_ext


def _rotary_tables(n_ctx, n_lat):
    def angles(rot_dim):
        n_freq = rot_dim // 4
        inv = ROPE_BASE ** (-(jnp.arange(n_freq, dtype=F32) / n_freq))
        rows = n_lat // GRID_W
        row = jnp.repeat(jnp.arange(rows, dtype=F32), GRID_W)
        col = jnp.tile(jnp.arange(GRID_W, dtype=F32), rows)
        theta = jnp.concatenate([row[:, None] * inv, col[:, None] * inv], axis=-1)
        theta = jnp.concatenate([jnp.zeros((n_ctx, 2 * n_freq), F32), theta], axis=0)
        return jnp.cos(theta), jnp.sin(theta)

    n = n_ctx + n_lat
    cm, sm = angles(MLA_ROPE)
    cd, sd = angles(DIFF_QK)
    zeros = lambda w: jnp.zeros((n, w), F32)
    ones = lambda w: jnp.ones((n, w), F32)
    pad_m = LANES - MLA_NOPE - MLA_ROPE
    log2e = 1.0 / math.log(2.0)
    scale_m = log2e / math.sqrt(MLA_NOPE + MLA_ROPE)
    scale_d = log2e / math.sqrt(DIFF_QK)
    cq = jnp.concatenate([ones(MLA_NOPE), cm, cm, zeros(pad_m)], axis=1)
    sq = jnp.concatenate([zeros(MLA_NOPE), -sm, sm, zeros(pad_m)], axis=1)
    ck = jnp.concatenate([zeros(MLA_NOPE), cm, cm, zeros(pad_m)], axis=1)
    cdd = jnp.concatenate([cd, cd, cd, cd], axis=1)
    sdd = jnp.concatenate([-sd, sd, -sd, sd], axis=1)
    lo = jnp.concatenate([ones(DIFF_QK), zeros(DIFF_QK)], axis=1)
    hi = 1.0 - lo
    return jnp.concatenate([cq * scale_m, sq * scale_m, ck, sq,
                            cdd * lo * scale_d, sdd * lo * scale_d, cdd * hi * scale_d, sdd * hi * scale_d,
                            cdd, sdd], axis=1)


def kernel(x, c, ctx, c_ctx, w_mod, b_mod, g_attn, g_ffn, w_in, g_q_lat, w_uq, g_kv_lat, w_ukv, lam_q1, lam_k1,
           lam_q2, lam_k2, g_subln, w_out, w_router, router_bias, w1, w3, w2, ws1, ws3, ws2, g_final):
    n_batch, n_lat, d = x.shape
    n_ctx = ctx.shape[1]
    t = n_batch * n_lat
    tm = 256
    tq = 512
    bm = 256
    td = 128
    assert w_mod.shape[0] == 1 and n_ctx % tm == 0 and n_lat % tq == 0 and n_lat % tm == 0 and t % td == 0
    assert (t * TOP_K) % bm == 0

    cc = jnp.concatenate([c, c_ctx[None, :], jnp.zeros((8 - n_batch - 1, d), F32)], axis=0)
    mod = _modulation(cc, w_mod[0], b_mod)

    w_in_ext, w_uq_ext, w_ukv_ext = _layout_weights(w_in[0], w_uq[0], w_ukv[0])
    tables = _rotary_tables(n_ctx, n_lat)
    qm, km, vmt, qd, k12, vdt = _projections(ctx, x, mod, g_attn, w_in_ext, g_q_lat, w_uq_ext, g_kv_lat,
                                             w_ukv_ext, tables, tm)
    om = _attn_mla(qm, km, vmt, tq)
    od = _attn_diff((lam_q1, lam_k1, lam_q2, lam_k2), g_subln.reshape(-1, 1), qd, k12, vdt, tq // 2)

    ws13 = jnp.concatenate([ws1[0], ws3[0]], axis=1).astype(BF16)
    xp, h2, eidx, rank, wts, cnt = _post(
        om.reshape(t, -1), od.reshape(t, -1), x.reshape(t, d), mod, w_out[0].astype(BF16), g_ffn, ws13,
        ws2[0].astype(BF16), w_router[0].T, router_bias[0][:, None], n_lat, tm)

    counts = cnt[:, 0]
    start = (jnp.cumsum(counts) - counts).astype(jnp.int32)
    n_slots = t * TOP_K
    n_blocks = n_slots // bm
    blk_b = jnp.arange(n_blocks, dtype=jnp.int32) * bm
    exp_b = jnp.concatenate([start[1:], jnp.full((1,), n_slots, jnp.int32)])
    pos_blk = jnp.arange(n_blocks, dtype=jnp.int32) + jnp.sum(exp_b[None, :] < blk_b[:, None], axis=1)
    pos_exp = jnp.arange(N_EXPERTS, dtype=jnp.int32) + jnp.sum(blk_b[None, :] <= exp_b[:, None], axis=1)
    slot = jnp.arange(n_blocks + N_EXPERTS, dtype=jnp.int32)[:, None]
    bounds = (jnp.sum(jnp.where(pos_blk[None, :] == slot, blk_b[None, :], 0), axis=1)
              + jnp.sum(jnp.where(pos_exp[None, :] == slot, exp_b[None, :], 0), axis=1))
    item_lo, item_hi = bounds[:-1], bounds[1:]
    item_blk = jnp.minimum(item_lo // bm, n_blocks - 1)
    item_e = jnp.clip(jnp.sum(start[None, :] <= item_lo[:, None], axis=1) - 1, 0, N_EXPERTS - 1).astype(jnp.int32)

    xs = _dispatch(start, h2, eidx, rank, td)
    ys = _experts(item_blk, item_e, item_lo, item_hi, xs, w1[0], w3[0], w2[0], bm)
    out = _combine(start, xp, wts.T, mod, g_final[None, :], eidx, rank, ys, n_lat, td)
    return out.reshape(n_batch, n_lat, d)
```

```python
import functools
import math

import jax
import jax.numpy as jnp
from jax import lax
from jax.experimental import pallas as pl
from jax.experimental.pallas import tpu as pltpu

GRID_W = 64
ROPE_BASE = 10000.0
NORM_EPS = 1e-6
MLA_HEADS = 8
MLA_NOPE = 64
MLA_ROPE = 32
MLA_V = 64
MLA_Q_RANK = 256
MLA_KV_RANK = 128
DIFF_HEADS = 4
DIFF_QK = 64
DIFF_V = 2 * DIFF_QK
N_EXPERTS = 256
TOP_K = 8
N_GROUPS = 8
TOPK_GROUPS = 4
ROUTED_SCALE = 2.5
LAM_INIT = 0.8 - 0.6 * math.exp(-0.3 * 0)

LANES = 128
VMEM_LIMIT = 48 * 1024 * 1024

F32 = jnp.float32
BF16 = jnp.bfloat16
NEG_INF = float("-inf")


def _cparams(sem):
    return pltpu.CompilerParams(dimension_semantics=sem, vmem_limit_bytes=VMEM_LIMIT)


def _rms_rows(x, g):
    return x * lax.rsqrt(jnp.mean(x * x, axis=-1, keepdims=True) + NORM_EPS) * g


def _dot(a, b):
    return jnp.dot(a, b, preferred_element_type=F32)


def _dot_nt(a, b):
    return lax.dot_general(a, b, (((1,), (1,)), ((), ())), preferred_element_type=F32)


def _mod_kernel(c_ref, w_ref, b_ref, o_ref):
    a = c_ref[...]
    a = a * jax.nn.sigmoid(a)
    o_ref[...] = jnp.dot(a, w_ref[...], preferred_element_type=F32,
                         precision=lax.Precision.HIGHEST) + b_ref[...]


def _modulation(cc, w_mod, b_mod):
    rows, d = cc.shape
    cols = w_mod.shape[1]
    tn = 1536
    return pl.pallas_call(
        _mod_kernel,
        out_shape=jax.ShapeDtypeStruct((rows, cols), F32),
        grid=(cols // tn,),
        in_specs=[pl.BlockSpec((rows, d), lambda j: (0, 0)),
                  pl.BlockSpec((d, tn), lambda j: (0, j)),
                  pl.BlockSpec((1, tn), lambda j: (0, j))],
        out_specs=pl.BlockSpec((rows, tn), lambda j: (0, j)),
        compiler_params=_cparams(("arbitrary",)),
        name="mod",
    )(cc, w_mod, b_mod)


_C_Q = 0
_C_KV = _C_Q + MLA_Q_RANK
_C_DQ = _C_KV + MLA_KV_RANK
_C_DQS = _C_DQ + DIFF_HEADS * 2 * DIFF_QK
_C_DK = _C_DQS + DIFF_HEADS * 2 * DIFF_QK
_C_DKS = _C_DK + DIFF_HEADS * 2 * DIFF_QK
_C_DV = _C_DKS + DIFF_HEADS * 2 * DIFF_QK
_C_KRA = _C_DV + DIFF_HEADS * DIFF_V
_C_KRB = _C_KRA + LANES
_IN_EXT = _C_KRB + LANES
_N_TAB = 10


def _proj_kernel(n_batch, d_model, ctx_ref, x_ref, mod_ref, g_attn_ref, w_in_ref, g_q_ref, w_uq_ref,
                 g_kv_ref, w_ukv_ref, tab_ref,
                 qm_ref, km_ref, vmt_ref, qd_ref, k12_ref, vdt_ref, *, n_ctx_tiles):
    b = pl.program_id(0)
    i = pl.program_id(1)
    is_ctx = i < n_ctx_tiles
    xin = jnp.where(is_ctx, ctx_ref[0], x_ref[0])
    row = jnp.where(is_ctx, n_batch, b)
    sh1 = mod_ref[pl.ds(row, 1), 0:d_model]
    sc1 = mod_ref[pl.ds(row, 1), d_model:2 * d_model]
    h = _rms_rows(xin, g_attn_ref[...]) * (1.0 + sc1) + sh1
    p = _dot(h.astype(BF16), w_in_ref[...])

    cq = _rms_rows(p[:, _C_Q:_C_Q + MLA_Q_RANK], g_q_ref[...])
    qa = _dot(cq.astype(BF16), w_uq_ref[...])
    ckv = _rms_rows(p[:, _C_KV:_C_KV + MLA_KV_RANK], g_kv_ref[...])
    kv = _dot(ckv.astype(BF16), w_ukv_ref[...])

    def tab(j):
        return tab_ref[:, j * LANES:(j + 1) * LANES]

    kr = p[:, _C_KRA:_C_KRA + LANES] * tab(2) + p[:, _C_KRB:_C_KRB + LANES] * tab(3)
    hw = MLA_HEADS * LANES
    for hd in range(MLA_HEADS):
        lo = hd * LANES
        qm_ref[0, hd] = (qa[:, lo:lo + LANES] * tab(0) + qa[:, hw + lo:hw + lo + LANES] * tab(1)).astype(BF16)
        km_ref[0, hd] = (kv[:, lo:lo + LANES] + kr).astype(BF16)
    for pr in range(MLA_HEADS // 2):
        lo = hw + pr * LANES
        vmt_ref[0, pr] = kv[:, lo:lo + LANES].T.astype(BF16)
    for hd in range(DIFF_HEADS):
        lo = hd * LANES
        dq = p[:, _C_DQ + lo:_C_DQ + lo + LANES]
        dqs = p[:, _C_DQS + lo:_C_DQS + lo + LANES]
        qd_ref[0, hd, 0] = (dq * tab(4) + dqs * tab(5)).astype(BF16)
        qd_ref[0, hd, 1] = (dq * tab(6) + dqs * tab(7)).astype(BF16)
        dk = p[:, _C_DK + lo:_C_DK + lo + LANES]
        dks = p[:, _C_DKS + lo:_C_DKS + lo + LANES]
        k12_ref[0, hd] = (dk * tab(8) + dks * tab(9)).astype(BF16)
        vdt_ref[0, hd] = p[:, _C_DV + lo:_C_DV + lo + LANES].T.astype(BF16)


def _projections(ctx, x, mod, g_attn, w_in_ext, g_q, w_uq_ext, g_kv, w_ukv_ext, tables, tm):
    n_batch, n_ctx, d = ctx.shape
    n_lat = x.shape[1]
    n_all = n_ctx + n_lat
    nct = n_ctx // tm
    nt = n_all // tm

    def full(a):
        return pl.BlockSpec(a.shape, lambda b, i: (0,) * a.ndim)

    def k_spec(heads):
        return pl.BlockSpec((1, heads, tm, LANES), lambda b, i: (b, 0, i, 0))

    def vt_spec(heads):
        return pl.BlockSpec((1, heads, LANES, tm), lambda b, i: (b, 0, 0, i))

    def lat(i):
        return jnp.maximum(i - nct, 0)

    def sds(*shape):
        return jax.ShapeDtypeStruct((n_batch,) + shape, BF16)

    return pl.pallas_call(
        functools.partial(_proj_kernel, n_batch, d, n_ctx_tiles=nct),
        out_shape=(sds(MLA_HEADS, n_lat, LANES), sds(MLA_HEADS, n_all, LANES), sds(MLA_HEADS // 2, LANES, n_all),
                   sds(DIFF_HEADS, 2, n_lat, LANES), sds(DIFF_HEADS, n_all, LANES),
                   sds(DIFF_HEADS, LANES, n_all)),
        grid=(n_batch, nt),
        in_specs=[pl.BlockSpec((1, tm, d), lambda b, i: (b, jnp.minimum(i, nct - 1), 0)),
                  pl.BlockSpec((1, tm, d), lambda b, i: (b, jnp.maximum(i - nct, 0), 0)),
                  full(mod), full(g_attn), full(w_in_ext), full(g_q), full(w_uq_ext), full(g_kv),
                  full(w_ukv_ext),
                  pl.BlockSpec((tm, _N_TAB * LANES), lambda b, i: (i, 0))],
        out_specs=(pl.BlockSpec((1, MLA_HEADS, tm, LANES), lambda b, i: (b, 0, lat(i), 0)),
                   k_spec(MLA_HEADS), vt_spec(MLA_HEADS // 2),
                   pl.BlockSpec((1, DIFF_HEADS, 2, tm, LANES), lambda b, i: (b, 0, 0, lat(i), 0)),
                   k_spec(DIFF_HEADS), vt_spec(DIFF_HEADS)),
        compiler_params=_cparams(("arbitrary", "arbitrary")),
        name="proj",
    )(ctx, x, mod, g_attn, w_in_ext, g_q, w_uq_ext, g_kv, w_ukv_ext, tables)


def _attend_t(q, k, vt):
    st = _dot_nt(k, q)
    et = jnp.exp2(st - jnp.max(st, axis=0, keepdims=True))
    l = jnp.sum(et, axis=0, keepdims=True)
    return _dot(vt, et.astype(BF16)) * (1.0 / l)


def _attn_mla_kernel(q_ref, k_ref, vt_ref, o_ref):
    sts = [_dot_nt(k_ref[0, j], q_ref[0, j]) for j in range(2)]
    ets = [jnp.exp2(st - jnp.max(st, axis=0, keepdims=True)) for st in sts]
    ls = [jnp.sum(et, axis=0, keepdims=True) for et in ets]
    outs = [_dot(vt_ref[0, 0, j * MLA_V:(j + 1) * MLA_V, :], ets[j].astype(BF16)) * (1.0 / ls[j]) for j in range(2)]
    o_ref[0] = jnp.concatenate(outs, axis=0).T.astype(BF16)


def _attn_mla(qm, km, vmt, tq):
    n_batch, heads, n_lat, _ = qm.shape
    n_all = km.shape[2]
    return pl.pallas_call(
        _attn_mla_kernel,
        out_shape=jax.ShapeDtypeStruct((n_batch, n_lat, heads // 2 * LANES), BF16),
        grid=(n_batch, heads // 2, n_lat // tq),
        in_specs=[pl.BlockSpec((1, 2, tq, LANES), lambda b, h, i: (b, h, i, 0)),
                  pl.BlockSpec((1, 2, n_all, LANES), lambda b, h, i: (b, h, 0, 0)),
                  pl.BlockSpec((1, 1, LANES, n_all), lambda b, h, i: (b, h, 0, 0))],
        out_specs=pl.BlockSpec((1, tq, LANES), lambda b, h, i: (b, i, h)),
        compiler_params=_cparams(("arbitrary", "arbitrary", "arbitrary")),
        name="attn_m",
    )(qm, km, vmt)


def _attn_diff_kernel(lq1_ref, lk1_ref, lq2_ref, lk2_ref, g_sub_ref, q_ref, k_ref, vt_ref, o_ref):
    lam = (jnp.exp(jnp.sum(lq1_ref[...] * lk1_ref[...], axis=-1, keepdims=True))
           - jnp.exp(jnp.sum(lq2_ref[...] * lk2_ref[...], axis=-1, keepdims=True)) + LAM_INIT)
    tq = q_ref.shape[3]
    o = _attend_t(q_ref[0, 0].reshape(2 * tq, LANES), k_ref[0, 0], vt_ref[0, 0])
    ot = o[:, :tq] - lam * o[:, tq:]
    ot = ot * lax.rsqrt(jnp.mean(ot * ot, axis=0, keepdims=True) + NORM_EPS) * g_sub_ref[...]
    o_ref[0] = (ot * (1.0 - LAM_INIT)).T.astype(BF16)


def _attn_diff(lams, g_sub_col, qd, k12, vdt, tq):
    n_batch, heads, _, n_lat, _ = qd.shape
    n_all = k12.shape[2]

    def small(a):
        return pl.BlockSpec(a.shape, lambda b, h, i: (0,) * a.ndim)

    return pl.pallas_call(
        _attn_diff_kernel,
        out_shape=jax.ShapeDtypeStruct((n_batch, n_lat, heads * LANES), BF16),
        grid=(n_batch, heads, n_lat // tq),
        in_specs=[small(lams[0]), small(lams[1]), small(lams[2]), small(lams[3]), small(g_sub_col),
                  pl.BlockSpec((1, 1, 2, tq, LANES), lambda b, h, i: (b, h, 0, i, 0)),
                  pl.BlockSpec((1, 1, n_all, LANES), lambda b, h, i: (b, h, 0, 0)),
                  pl.BlockSpec((1, 1, LANES, n_all), lambda b, h, i: (b, h, 0, 0))],
        out_specs=pl.BlockSpec((1, tq, LANES), lambda b, h, i: (b, i, h)),
        compiler_params=_cparams(("arbitrary", "arbitrary", "arbitrary")),
        name="attn_d",
    )(*lams, g_sub_col, qd, k12, vdt)


def _post_kernel(d_model, tiles_per_batch, om_ref, od_ref, x_ref, mod_ref, w_out_ref, g_ffn_ref, ws13_ref,
                 ws2_ref, wr_ref, br_ref,
                 xp_ref, h2p_ref, eidx_ref, rank_ref, wts_ref, cnt_ref, carry_ref):
    i = pl.program_id(0)
    tm = x_ref.shape[0]
    n_exp = wr_ref.shape[0]
    per_group = n_exp // N_GROUPS

    @pl.when(i == 0)
    def _():
        carry_ref[...] = jnp.zeros_like(carry_ref)

    b = i // tiles_per_batch

    def modv(j):
        return mod_ref[pl.ds(b, 1), j * d_model:(j + 1) * d_model]

    half = om_ref.shape[1]
    y = _dot(om_ref[...], w_out_ref[0:half, :]) + _dot(od_ref[...], w_out_ref[half:, :])
    x1 = x_ref[...] + modv(2) * y
    h2 = _rms_rows(x1, g_ffn_ref[...]) * (1.0 + modv(4)) + modv(3)
    h2p_ref[...] = _pack_pair(h2[:, :d_model // 2], h2[:, d_model // 2:])

    h2b = h2.astype(BF16)
    gu = _dot(h2b, ws13_ref[...])
    ff = gu.shape[1] // 2
    g = gu[:, :ff]
    act = g * jax.nn.sigmoid(g) * gu[:, ff:]
    shared = _dot(act.astype(BF16), ws2_ref[...])
    xp_ref[...] = x1 + modv(5) * shared

    logits = lax.dot_general(wr_ref[...], h2, (((1,), (1,)), ((), ())), preferred_element_type=F32,
                             precision=lax.Precision.HIGHEST)
    s = jax.nn.sigmoid(logits)
    ssel = s + br_ref[...]
    s3 = ssel.reshape(N_GROUPS, per_group, tm)
    m1 = jnp.max(s3, axis=1, keepdims=True)
    eq = s3 == m1
    n_eq = jnp.sum(jnp.where(eq, 1.0, 0.0), axis=1, keepdims=True)
    m2 = jnp.max(jnp.where(eq, NEG_INF, s3), axis=1, keepdims=True)
    grp = m1 + jnp.where(n_eq >= 2.0, m1, m2)

    gi = lax.broadcasted_iota(jnp.int32, grp.shape, 0).astype(F32)
    gcur = grp
    gsel = jnp.zeros_like(grp)
    for _ in range(TOPK_GROUPS):
        gm = jnp.max(gcur, axis=0, keepdims=True)
        first = jnp.min(jnp.where(gcur == gm, gi, float(N_GROUPS)), axis=0, keepdims=True)
        oh = gi == first
        gsel = jnp.where(oh, 1.0, gsel)
        gcur = jnp.where(oh, NEG_INF, gcur)
    cur = jnp.where(gsel > 0.0, s3, NEG_INF).reshape(n_exp, tm)

    ie = lax.broadcasted_iota(jnp.int32, (n_exp, tm), 0).astype(F32)
    sel = jnp.zeros((n_exp, tm), F32)
    e_rows = []
    s_rows = []
    for _ in range(TOP_K):
        m = jnp.max(cur, axis=0, keepdims=True)
        first = jnp.min(jnp.where(cur == m, ie, float(n_exp)), axis=0, keepdims=True)
        oh = ie == first
        cur = jnp.where(oh, NEG_INF, cur)
        sel = jnp.where(oh, 1.0, sel)
        e_rows.append(first)
        s_rows.append(jnp.sum(jnp.where(oh, s, 0.0), axis=0, keepdims=True))
    s_tot = s_rows[0]
    for r in s_rows[1:]:
        s_tot = s_tot + r

    tr = lax.broadcasted_iota(jnp.int32, (tm, tm), 0)
    tc = lax.broadcasted_iota(jnp.int32, (tm, tm), 1)
    upper = jnp.where(tr < tc, 1.0, 0.0).astype(BF16)
    rank = _dot(sel.astype(BF16), upper) + carry_ref[...]
    for k in range(TOP_K):
        eidx_ref[k:k + 1, :] = e_rows[k].astype(jnp.int32)
        rank_ref[k:k + 1, :] = jnp.sum(jnp.where(ie == e_rows[k], rank, 0.0), axis=0,
                                       keepdims=True).astype(jnp.int32)
        wts_ref[k:k + 1, :] = s_rows[k] / s_tot * ROUTED_SCALE
    carry_ref[...] = carry_ref[...] + jnp.sum(sel, axis=1, keepdims=True)
    cnt_ref[...] = carry_ref[...].astype(jnp.int32)


def _post(om, od, x2d, mod, w_out, g_ffn, ws13, ws2, wr_t, br, n_lat, tm):
    t, d = x2d.shape
    n_exp = wr_t.shape[0]

    def full(a):
        return pl.BlockSpec(a.shape, lambda i: (0,) * a.ndim)

    row = lambda w: pl.BlockSpec((tm, w), lambda i: (i, 0))
    col = pl.BlockSpec((TOP_K, tm), lambda i: (0, i))
    return pl.pallas_call(
        functools.partial(_post_kernel, d, n_lat // tm),
        out_shape=(jax.ShapeDtypeStruct((t, d), F32), jax.ShapeDtypeStruct((t, d // 2), jnp.uint32),
                   jax.ShapeDtypeStruct((TOP_K, t), jnp.int32), jax.ShapeDtypeStruct((TOP_K, t), jnp.int32),
                   jax.ShapeDtypeStruct((TOP_K, t), F32), jax.ShapeDtypeStruct((n_exp, 1), jnp.int32)),
        grid=(t // tm,),
        in_specs=[row(om.shape[1]), row(od.shape[1]), row(d), full(mod), full(w_out), full(g_ffn),
                  full(ws13), full(ws2), full(wr_t), full(br)],
        out_specs=(row(d), row(d // 2), col, col, col, pl.BlockSpec((n_exp, 1), lambda i: (0, 0))),
        scratch_shapes=[pltpu.VMEM((n_exp, 1), F32)],
        compiler_params=_cparams(("arbitrary",)),
        name="post",
    )(om, od, x2d, mod, w_out, g_ffn, ws13, ws2, wr_t, br)


_HI_MASK = 0xFFFF0000


def _pack_pair(a, b):
    lo = lax.bitcast_convert_type(a.astype(BF16).astype(F32), jnp.uint32) >> 16
    hi = lax.bitcast_convert_type(b.astype(BF16).astype(F32), jnp.uint32) & jnp.uint32(_HI_MASK)
    return lo | hi


def _unpack_pair(w):
    return (lax.bitcast_convert_type(w << 16, F32),
            lax.bitcast_convert_type(w & jnp.uint32(_HI_MASK), F32))


def _row_copy(src, src_row, dst, dst_row, sem):
    return pltpu.make_async_copy(src.at[pl.ds(src_row, 1), :], dst.at[pl.ds(dst_row, 1), :], sem)


def _routing_copies(t, td, eidx_hbm, rank_hbm, e_s, r_s, sem, buf):
    return (pltpu.make_async_copy(eidx_hbm.at[:, pl.ds(t * td, td)], e_s.at[buf], sem.at[buf, 0]),
            pltpu.make_async_copy(rank_hbm.at[:, pl.ds(t * td, td)], r_s.at[buf], sem.at[buf, 1]))


def _dispatch_kernel(start_ref, h2p_hbm, eidx_hbm, rank_hbm, xs_hbm, hbuf, e_s, r_s, fsem, rsem, dsem):
    i = pl.program_id(0)
    n = pl.num_programs(0)
    td = hbuf.shape[1]
    buf = i % 2

    def fetch(t):
        slot = t % 3
        return pltpu.make_async_copy(h2p_hbm.at[pl.ds(t * td, td), :], hbuf.at[slot], fsem.at[slot])

    def routing(t):
        return _routing_copies(t, td, eidx_hbm, rank_hbm, e_s, r_s, rsem, t % 2)

    def wait_rows(t):
        def body(j, c):
            for k in range(TOP_K):
                _row_copy(hbuf.at[0], 0, xs_hbm, 0, dsem.at[t % 2]).wait()
            return c
        lax.fori_loop(0, td, body, 0)

    @pl.when(i == 0)
    def _():
        fetch(0).start()
        for cp in routing(0):
            cp.start()

        @pl.when(n > 1)
        def _():
            fetch(1).start()

    @pl.when(i + 1 < n)
    def _():
        for cp in routing(i + 1):
            cp.start()

    fetch(i).wait()
    for cp in routing(i):
        cp.wait()

    def issue(j, c):
        for k in range(TOP_K):
            dst = start_ref[e_s[buf, k, j]] + r_s[buf, k, j]
            _row_copy(hbuf.at[i % 3], j, xs_hbm, dst, dsem.at[buf]).start()
        return c

    lax.fori_loop(0, td, issue, 0)

    @pl.when(i >= 1)
    def _():
        wait_rows(i - 1)

    @pl.when(i + 2 < n)
    def _():
        fetch(i + 2).start()

    @pl.when(i == n - 1)
    def _():
        wait_rows(i)


def _dispatch(start, h2p, eidx, rank, td):
    t, dw = h2p.shape
    any_spec = pl.BlockSpec(memory_space=pl.ANY)
    return pl.pallas_call(
        _dispatch_kernel,
        out_shape=jax.ShapeDtypeStruct((t * TOP_K, dw), jnp.uint32),
        grid_spec=pltpu.PrefetchScalarGridSpec(
            num_scalar_prefetch=1,
            grid=(t // td,),
            in_specs=[any_spec, any_spec, any_spec],
            out_specs=any_spec,
            scratch_shapes=[pltpu.VMEM((3, td, dw), jnp.uint32),
                            pltpu.SMEM((2, TOP_K, td), jnp.int32), pltpu.SMEM((2, TOP_K, td), jnp.int32),
                            pltpu.SemaphoreType.DMA((3,)), pltpu.SemaphoreType.DMA((2, 2)),
                            pltpu.SemaphoreType.DMA((2,))]),
        compiler_params=_cparams(("arbitrary",)),
        name="dispatch",
    )(start, h2p, eidx, rank)


def _experts_kernel(blk_ref, e_ref, lo_ref, hi_ref, xs_ref, w1_ref, w3_ref, w2_ref, ys_ref, w13_b, w2_b):
    j = pl.program_id(0)
    lo = lo_ref[j]
    hi = hi_ref[j]
    bm, dw = xs_ref.shape
    ff = w1_ref.shape[2]
    base = blk_ref[j] * bm

    @pl.when(jnp.logical_or(j == 0, e_ref[j] != e_ref[jnp.maximum(j - 1, 0)]))
    def _():
        w13_b[:, :ff] = w1_ref[0].astype(BF16)
        w13_b[:, ff:] = w3_ref[0].astype(BF16)
        w2_b[...] = w2_ref[0].astype(BF16)

    def swiglu():
        x_lo, x_hi = _unpack_pair(xs_ref[...])
        gu = _dot(x_lo.astype(BF16), w13_b[:dw, :]) + _dot(x_hi.astype(BF16), w13_b[dw:, :])
        g = gu[:, :ff]
        act = g * jax.nn.sigmoid(g) * gu[:, ff:]
        y = _dot(act.astype(BF16), w2_b[...])
        return _pack_pair(y[:, :dw], y[:, dw:])

    @pl.when(jnp.logical_and(hi > lo, lo == base))
    def _():
        ys_ref[...] = swiglu()

    @pl.when(jnp.logical_and(hi > lo, lo != base))
    def _():
        rows = base + lax.broadcasted_iota(jnp.int32, (bm, 1), 0)
        ys_ref[...] = jnp.where(rows >= lo, swiglu(), ys_ref[...])


def _experts(item_blk, item_e, item_lo, item_hi, xs, w1, w3, w2, bm):
    n_slots, dw = xs.shape
    _, d, ff = w1.shape
    return pl.pallas_call(
        _experts_kernel,
        out_shape=jax.ShapeDtypeStruct((n_slots, dw), jnp.uint32),
        grid_spec=pltpu.PrefetchScalarGridSpec(
            num_scalar_prefetch=4,
            grid=(item_blk.shape[0],),
            in_specs=[pl.BlockSpec((bm, dw), lambda j, blk, e, lo, hi: (blk[j], 0)),
                      pl.BlockSpec((1, d, ff), lambda j, blk, e, lo, hi: (e[j], 0, 0)),
                      pl.BlockSpec((1, d, ff), lambda j, blk, e, lo, hi: (e[j], 0, 0)),
                      pl.BlockSpec((1, ff, d), lambda j, blk, e, lo, hi: (e[j], 0, 0))],
            out_specs=pl.BlockSpec((bm, dw), lambda j, blk, e, lo, hi: (blk[j], 0)),
            scratch_shapes=[pltpu.VMEM((d, 2 * ff), BF16), pltpu.VMEM((ff, d), BF16)]),
        compiler_params=_cparams(("arbitrary",)),
        name="experts",
    )(item_blk, item_e, item_lo, item_hi, xs, w1, w3, w2)


def _combine_kernel(d_model, tiles_per_batch, start_ref, xp_ref, wts_ref, mod_ref, g_final_ref, eidx_hbm,
                    rank_hbm, ys_hbm, o_ref, gbuf, e_s, r_s, rsem, gsem):
    i = pl.program_id(0)
    n = pl.num_programs(0)
    tc = xp_ref.shape[0]
    dw = d_model // 2
    buf = i % 2

    def routing(t):
        return _routing_copies(t, tc, eidx_hbm, rank_hbm, e_s, r_s, rsem, t % 2)

    def issue(t):
        tb = t % 2

        def body(j, c):
            for k in range(TOP_K):
                src = start_ref[e_s[tb, k, j]] + r_s[tb, k, j]
                _row_copy(ys_hbm, src, gbuf.at[tb, k], j, gsem.at[tb]).start()
            return c
        lax.fori_loop(0, tc, body, 0)

    @pl.when(i == 0)
    def _():
        for cp in routing(0):
            cp.start()
        for cp in routing(0):
            cp.wait()
        issue(0)

        @pl.when(n > 1)
        def _():
            for cp in routing(1):
                cp.start()

    @pl.when(i + 1 < n)
    def _():
        for cp in routing(i + 1):
            cp.wait()
        issue(i + 1)

    @pl.when(i + 2 < n)
    def _():
        for cp in routing(i + 2):
            cp.start()

    def wait(j, c):
        for k in range(TOP_K):
            _row_copy(ys_hbm, 0, gbuf.at[buf, k], 0, gsem.at[buf]).wait()
        return c

    lax.fori_loop(0, tc, wait, 0)

    w = wts_ref[...]
    r_lo = r_hi = None
    for k in range(TOP_K):
        y_lo, y_hi = _unpack_pair(gbuf[buf, k])
        wk = w[:, k:k + 1]
        r_lo = wk * y_lo if r_lo is None else r_lo + wk * y_lo
        r_hi = wk * y_hi if r_hi is None else r_hi + wk * y_hi
    b = i // tiles_per_batch
    gt2 = mod_ref[pl.ds(b, 1), 5 * d_model:6 * d_model]
    v_lo = xp_ref[:, :dw] + gt2[:, :dw] * r_lo
    v_hi = xp_ref[:, dw:] + gt2[:, dw:] * r_hi
    ms = (jnp.sum(v_lo * v_lo, axis=-1, keepdims=True) + jnp.sum(v_hi * v_hi, axis=-1, keepdims=True)) / d_model
    inv = lax.rsqrt(ms + NORM_EPS)
    o_ref[:, :dw] = v_lo * inv * g_final_ref[:, :dw]
    o_ref[:, dw:] = v_hi * inv * g_final_ref[:, dw:]


def _combine(start, xp, wts_t, mod, g_final, eidx, rank, ys, n_lat, tc):
    t, d = xp.shape
    any_spec = pl.BlockSpec(memory_space=pl.ANY)
    return pl.pallas_call(
        functools.partial(_combine_kernel, d, n_lat // tc),
        out_shape=jax.ShapeDtypeStruct((t, d), F32),
        grid_spec=pltpu.PrefetchScalarGridSpec(
            num_scalar_prefetch=1,
            grid=(t // tc,),
            in_specs=[pl.BlockSpec((tc, d), lambda i, *_: (i, 0)),
                      pl.BlockSpec((tc, TOP_K), lambda i, *_: (i, 0)),
                      pl.BlockSpec(mod.shape, lambda i, *_: (0, 0)),
                      pl.BlockSpec(g_final.shape, lambda i, *_: (0, 0)),
                      any_spec, any_spec, any_spec],
            out_specs=pl.BlockSpec((tc, d), lambda i, *_: (i, 0)),
            scratch_shapes=[pltpu.VMEM((2, TOP_K, tc, d // 2), jnp.uint32),
                            pltpu.SMEM((2, TOP_K, tc), jnp.int32), pltpu.SMEM((2, TOP_K, tc), jnp.int32),
                            pltpu.SemaphoreType.DMA((2, 2)), pltpu.SemaphoreType.DMA((2,))]),
        compiler_params=_cparams(("arbitrary",)),
        name="combine",
    )(start, xp, wts_t, mod, g_final, eidx, rank, ys)


def _swap_halves(w, width):
    r, c = w.shape
    return w.reshape(r, c // width, 2, width // 2)[:, :, ::-1, :].reshape(r, c)


def _pad_cols(w, left, total):
    return jnp.pad(w, ((0, 0), (left, total - left - w.shape[1])))


def _layout_weights(w_in, w_uq, w_ukv):
    s0 = MLA_Q_RANK
    s1 = s0 + MLA_KV_RANK
    s2 = s1 + MLA_ROPE
    s3 = s2 + DIFF_HEADS * 2 * DIFF_QK
    s4 = s3 + DIFF_HEADS * 2 * DIFF_QK
    w_kr, w_dq, w_dk = w_in[:, s1:s2], w_in[:, s2:s3], w_in[:, s3:s4]
    w_in_ext = jnp.concatenate([
        w_in[:, :s1], w_dq, _swap_halves(w_dq, DIFF_QK), w_dk, _swap_halves(w_dk, DIFF_QK), w_in[:, s4:],
        _pad_cols(w_kr, MLA_NOPE, LANES), _pad_cols(_swap_halves(w_kr, MLA_ROPE), MLA_NOPE, LANES)],
        axis=1).astype(BF16)

    qk = MLA_NOPE + MLA_ROPE
    qa, qb, kk, vv = [], [], [], []
    for h in range(MLA_HEADS):
        wq = w_uq[:, h * qk:(h + 1) * qk]
        qa.append(_pad_cols(wq, 0, LANES))
        qb.append(_pad_cols(_swap_halves(wq[:, MLA_NOPE:], MLA_ROPE), MLA_NOPE, LANES))
        wkv = w_ukv[:, h * (MLA_NOPE + MLA_V):(h + 1) * (MLA_NOPE + MLA_V)]
        kk.append(_pad_cols(wkv[:, :MLA_NOPE], 0, LANES))
        vv.append(wkv[:, MLA_NOPE:])
    w_uq_ext = jnp.concatenate(qa + qb, axis=1).astype(BF16)
    w_ukv_ext = jnp.concatenate(kk + vv, axis=1).astype(BF16)
    return w_in_ext, w_uq_ext, w_ukv_ext


def _rotary_tables(n_ctx, n_lat):
    def angles(rot_dim):
        n_freq = rot_dim // 4
        inv = ROPE_BASE ** (-(jnp.arange(n_freq, dtype=F32) / n_freq))
        rows = n_lat // GRID_W
        row = jnp.repeat(jnp.arange(rows, dtype=F32), GRID_W)
        col = jnp.tile(jnp.arange(GRID_W, dtype=F32), rows)
        theta = jnp.concatenate([row[:, None] * inv, col[:, None] * inv], axis=-1)
        theta = jnp.concatenate([jnp.zeros((n_ctx, 2 * n_freq), F32), theta], axis=0)
        return jnp.cos(theta), jnp.sin(theta)

    n = n_ctx + n_lat
    cm, sm = angles(MLA_ROPE)
    cd, sd = angles(DIFF_QK)
    zeros = lambda w: jnp.zeros((n, w), F32)
    ones = lambda w: jnp.ones((n, w), F32)
    pad_m = LANES - MLA_NOPE - MLA_ROPE
    log2e = 1.0 / math.log(2.0)
    scale_m = log2e / math.sqrt(MLA_NOPE + MLA_ROPE)
    scale_d = log2e / math.sqrt(DIFF_QK)
    cq = jnp.concatenate([ones(MLA_NOPE), cm, cm, zeros(pad_m)], axis=1)
    sq = jnp.concatenate([zeros(MLA_NOPE), -sm, sm, zeros(pad_m)], axis=1)
    ck = jnp.concatenate([zeros(MLA_NOPE), cm, cm, zeros(pad_m)], axis=1)
    cdd = jnp.concatenate([cd, cd, cd, cd], axis=1)
    sdd = jnp.concatenate([-sd, sd, -sd, sd], axis=1)
    lo = jnp.concatenate([ones(DIFF_QK), zeros(DIFF_QK)], axis=1)
    hi = 1.0 - lo
    return jnp.concatenate([cq * scale_m, sq * scale_m, ck, sq,
                            cdd * lo * scale_d, sdd * lo * scale_d, cdd * hi * scale_d, sdd * hi * scale_d,
                            cdd, sdd], axis=1)


def kernel(x, c, ctx, c_ctx, w_mod, b_mod, g_attn, g_ffn, w_in, g_q_lat, w_uq, g_kv_lat, w_ukv, lam_q1, lam_k1,
           lam_q2, lam_k2, g_subln, w_out, w_router, router_bias, w1, w3, w2, ws1, ws3, ws2, g_final):
    n_batch, n_lat, d = x.shape
    n_ctx = ctx.shape[1]
    t = n_batch * n_lat
    tm = 256
    tq = 512
    bm = 256
    td = 128
    assert w_mod.shape[0] == 1 and n_ctx % tm == 0 and n_lat % tq == 0 and n_lat % tm == 0 and t % td == 0
    assert (t * TOP_K) % bm == 0

    cc = jnp.concatenate([c, c_ctx[None, :], jnp.zeros((8 - n_batch - 1, d), F32)], axis=0)
    mod = _modulation(cc, w_mod[0], b_mod)

    w_in_ext, w_uq_ext, w_ukv_ext = _layout_weights(w_in[0], w_uq[0], w_ukv[0])
    tables = _rotary_tables(n_ctx, n_lat)
    qm, km, vmt, qd, k12, vdt = _projections(ctx, x, mod, g_attn, w_in_ext, g_q_lat, w_uq_ext, g_kv_lat,
                                             w_ukv_ext, tables, tm)
    om = _attn_mla(qm, km, vmt, tq)
    od = _attn_diff((lam_q1, lam_k1, lam_q2, lam_k2), g_subln.reshape(-1, 1), qd, k12, vdt, tq // 2)

    ws13 = jnp.concatenate([ws1[0], ws3[0]], axis=1).astype(BF16)
    xp, h2p, eidx, rank, wts, cnt = _post(
        om.reshape(t, -1), od.reshape(t, -1), x.reshape(t, d), mod, w_out[0].astype(BF16), g_ffn, ws13,
        ws2[0].astype(BF16), w_router[0].T, router_bias[0][:, None], n_lat, tm)

    counts = cnt[:, 0]
    start = (jnp.cumsum(counts) - counts).astype(jnp.int32)
    n_slots = t * TOP_K
    n_blocks = n_slots // bm
    blk_b = jnp.arange(n_blocks, dtype=jnp.int32) * bm
    exp_b = jnp.concatenate([start[1:], jnp.full((1,), n_slots, jnp.int32)])
    pos_blk = jnp.arange(n_blocks, dtype=jnp.int32) + jnp.sum(exp_b[None, :] < blk_b[:, None], axis=1)
    pos_exp = jnp.arange(N_EXPERTS, dtype=jnp.int32) + jnp.sum(blk_b[None, :] <= exp_b[:, None], axis=1)
    slot = jnp.arange(n_blocks + N_EXPERTS, dtype=jnp.int32)[:, None]
    bounds = (jnp.sum(jnp.where(pos_blk[None, :] == slot, blk_b[None, :], 0), axis=1)
              + jnp.sum(jnp.where(pos_exp[None, :] == slot, exp_b[None, :], 0), axis=1))
    item_lo, item_hi = bounds[:-1], bounds[1:]
    item_blk = jnp.minimum(item_lo // bm, n_blocks - 1)
    item_e = jnp.clip(jnp.sum(start[None, :] <= item_lo[:, None], axis=1) - 1, 0, N_EXPERTS - 1).astype(jnp.int32)

    xs = _dispatch(start, h2p, eidx, rank, td)
    ys = _experts(item_blk, item_e, item_lo, item_hi, xs, w1[0], w3[0], w2[0], bm)
    out = _combine(start, xp, wts.T, mod, g_final[None, :], eidx, rank, ys, n_lat, td)
    return out.reshape(n_batch, n_lat, d)
```

```python
import functools
import math

import jax
import jax.numpy as jnp
from jax import lax
from jax.experimental import pallas as pl
from jax.experimental.pallas import tpu as pltpu

GRID_W = 64
ROPE_BASE = 10000.0
NORM_EPS = 1e-6
MLA_HEADS = 8
MLA_NOPE = 64
MLA_ROPE = 32
MLA_V = 64
MLA_Q_RANK = 256
MLA_KV_RANK = 128
DIFF_HEADS = 4
DIFF_QK = 64
DIFF_V = 2 * DIFF_QK
N_EXPERTS = 256
TOP_K = 8
N_GROUPS = 8
TOPK_GROUPS = 4
ROUTED_SCALE = 2.5
LAM_INIT = 0.8 - 0.6 * math.exp(-0.3 * 0)

LANES = 128
VMEM_LIMIT = 48 * 1024 * 1024

F32 = jnp.float32
BF16 = jnp.bfloat16
NEG_INF = float("-inf")


def _cparams(sem):
    return pltpu.CompilerParams(dimension_semantics=sem, vmem_limit_bytes=VMEM_LIMIT)


def _rms_rows(x, g):
    return x * lax.rsqrt(jnp.mean(x * x, axis=-1, keepdims=True) + NORM_EPS) * g


def _dot(a, b):
    return jnp.dot(a, b, preferred_element_type=F32)


def _dot_nt(a, b):
    return lax.dot_general(a, b, (((1,), (1,)), ((), ())), preferred_element_type=F32)


def _mod_kernel(c_ref, w_ref, b_ref, o_ref):
    a = c_ref[...]
    a = a * jax.nn.sigmoid(a)
    o_ref[...] = jnp.dot(a, w_ref[...], preferred_element_type=F32,
                         precision=lax.Precision.HIGHEST) + b_ref[...]


def _modulation(cc, w_mod, b_mod):
    rows, d = cc.shape
    cols = w_mod.shape[1]
    tn = 1536
    return pl.pallas_call(
        _mod_kernel,
        out_shape=jax.ShapeDtypeStruct((rows, cols), F32),
        grid=(cols // tn,),
        in_specs=[pl.BlockSpec((rows, d), lambda j: (0, 0)),
                  pl.BlockSpec((d, tn), lambda j: (0, j)),
                  pl.BlockSpec((1, tn), lambda j: (0, j))],
        out_specs=pl.BlockSpec((rows, tn), lambda j: (0, j)),
        compiler_params=_cparams(("arbitrary",)),
        name="mod",
    )(cc, w_mod, b_mod)


_C_Q = 0
_C_KV = _C_Q + MLA_Q_RANK
_C_DQ = _C_KV + MLA_KV_RANK
_C_DQS = _C_DQ + DIFF_HEADS * 2 * DIFF_QK
_C_DK = _C_DQS + DIFF_HEADS * 2 * DIFF_QK
_C_DKS = _C_DK + DIFF_HEADS * 2 * DIFF_QK
_C_DV = _C_DKS + DIFF_HEADS * 2 * DIFF_QK
_C_KRA = _C_DV + DIFF_HEADS * DIFF_V
_C_KRB = _C_KRA + LANES
_IN_EXT = _C_KRB + LANES
_N_TAB = 10


def _proj_kernel(n_batch, d_model, ctx_ref, x_ref, mod_ref, g_attn_ref, w_in_ref, g_q_ref, w_uq_ref,
                 g_kv_ref, w_ukv_ref, tab_ref,
                 qm_ref, km_ref, vmt_ref, qd_ref, k12_ref, vdt_ref, *, n_ctx_tiles):
    b = pl.program_id(0)
    i = pl.program_id(1)
    is_ctx = i < n_ctx_tiles
    xin = jnp.where(is_ctx, ctx_ref[0], x_ref[0])
    row = jnp.where(is_ctx, n_batch, b)
    sh1 = mod_ref[pl.ds(row, 1), 0:d_model]
    sc1 = mod_ref[pl.ds(row, 1), d_model:2 * d_model]
    h = _rms_rows(xin, g_attn_ref[...]) * (1.0 + sc1) + sh1
    p = _dot(h.astype(BF16), w_in_ref[...])

    cq = _rms_rows(p[:, _C_Q:_C_Q + MLA_Q_RANK], g_q_ref[...])
    qa = _dot(cq.astype(BF16), w_uq_ref[...])
    ckv = _rms_rows(p[:, _C_KV:_C_KV + MLA_KV_RANK], g_kv_ref[...])
    kv = _dot(ckv.astype(BF16), w_ukv_ref[...])

    def tab(j):
        return tab_ref[:, j * LANES:(j + 1) * LANES]

    kr = p[:, _C_KRA:_C_KRA + LANES] * tab(2) + p[:, _C_KRB:_C_KRB + LANES] * tab(3)
    hw = MLA_HEADS * LANES
    for hd in range(MLA_HEADS):
        lo = hd * LANES
        qm_ref[0, hd] = (qa[:, lo:lo + LANES] * tab(0) + qa[:, hw + lo:hw + lo + LANES] * tab(1)).astype(BF16)
        km_ref[0, hd] = (kv[:, lo:lo + LANES] + kr).astype(BF16)
    for pr in range(MLA_HEADS // 2):
        lo = hw + pr * LANES
        vmt_ref[0, pr] = kv[:, lo:lo + LANES].T.astype(BF16)
    for hd in range(DIFF_HEADS):
        lo = hd * LANES
        dq = p[:, _C_DQ + lo:_C_DQ + lo + LANES]
        dqs = p[:, _C_DQS + lo:_C_DQS + lo + LANES]
        qd_ref[0, hd, 0] = (dq * tab(4) + dqs * tab(5)).astype(BF16)
        qd_ref[0, hd, 1] = (dq * tab(6) + dqs * tab(7)).astype(BF16)
        dk = p[:, _C_DK + lo:_C_DK + lo + LANES]
        dks = p[:, _C_DKS + lo:_C_DKS + lo + LANES]
        k12_ref[0, hd] = (dk * tab(8) + dks * tab(9)).astype(BF16)
        vdt_ref[0, hd] = p[:, _C_DV + lo:_C_DV + lo + LANES].T.astype(BF16)


def _projections(ctx, x, mod, g_attn, w_in_ext, g_q, w_uq_ext, g_kv, w_ukv_ext, tables, tm):
    n_batch, n_ctx, d = ctx.shape
    n_lat = x.shape[1]
    n_all = n_ctx + n_lat
    nct = n_ctx // tm
    nt = n_all // tm

    def full(a):
        return pl.BlockSpec(a.shape, lambda b, i: (0,) * a.ndim)

    def k_spec(heads):
        return pl.BlockSpec((1, heads, tm, LANES), lambda b, i: (b, 0, i, 0))

    def vt_spec(heads):
        return pl.BlockSpec((1, heads, LANES, tm), lambda b, i: (b, 0, 0, i))

    def lat(i):
        return jnp.maximum(i - nct, 0)

    def sds(*shape):
        return jax.ShapeDtypeStruct((n_batch,) + shape, BF16)

    return pl.pallas_call(
        functools.partial(_proj_kernel, n_batch, d, n_ctx_tiles=nct),
        out_shape=(sds(MLA_HEADS, n_lat, LANES), sds(MLA_HEADS, n_all, LANES), sds(MLA_HEADS // 2, LANES, n_all),
                   sds(DIFF_HEADS, 2, n_lat, LANES), sds(DIFF_HEADS, n_all, LANES),
                   sds(DIFF_HEADS, LANES, n_all)),
        grid=(n_batch, nt),
        in_specs=[pl.BlockSpec((1, tm, d), lambda b, i: (b, jnp.minimum(i, nct - 1), 0)),
                  pl.BlockSpec((1, tm, d), lambda b, i: (b, jnp.maximum(i - nct, 0), 0)),
                  full(mod), full(g_attn), full(w_in_ext), full(g_q), full(w_uq_ext), full(g_kv),
                  full(w_ukv_ext),
                  pl.BlockSpec((tm, _N_TAB * LANES), lambda b, i: (i, 0))],
        out_specs=(pl.BlockSpec((1, MLA_HEADS, tm, LANES), lambda b, i: (b, 0, lat(i), 0)),
                   k_spec(MLA_HEADS), vt_spec(MLA_HEADS // 2),
                   pl.BlockSpec((1, DIFF_HEADS, 2, tm, LANES), lambda b, i: (b, 0, 0, lat(i), 0)),
                   k_spec(DIFF_HEADS), vt_spec(DIFF_HEADS)),
        compiler_params=_cparams(("arbitrary", "arbitrary")),
        name="proj",
    )(ctx, x, mod, g_attn, w_in_ext, g_q, w_uq_ext, g_kv, w_ukv_ext, tables)


def _attend_t(q, k, vt):
    st = _dot_nt(k, q)
    et = jnp.exp2(st - jnp.max(st, axis=0, keepdims=True))
    l = jnp.sum(et, axis=0, keepdims=True)
    return _dot(vt, et.astype(BF16)) * (1.0 / l)


def _attn_mla_kernel(q_ref, k_ref, vt_ref, o_ref):
    sts = [_dot_nt(k_ref[0, j], q_ref[0, j]) for j in range(2)]
    ets = [jnp.exp2(st - jnp.max(st, axis=0, keepdims=True)) for st in sts]
    ls = [jnp.sum(et, axis=0, keepdims=True) for et in ets]
    outs = [_dot(vt_ref[0, 0, j * MLA_V:(j + 1) * MLA_V, :], ets[j].astype(BF16)) * (1.0 / ls[j]) for j in range(2)]
    o_ref[0] = jnp.concatenate(outs, axis=0).T.astype(BF16)


def _attn_mla(qm, km, vmt, tq):
    n_batch, heads, n_lat, _ = qm.shape
    n_all = km.shape[2]
    return pl.pallas_call(
        _attn_mla_kernel,
        out_shape=jax.ShapeDtypeStruct((n_batch, n_lat, heads // 2 * LANES), BF16),
        grid=(n_batch, heads // 2, n_lat // tq),
        in_specs=[pl.BlockSpec((1, 2, tq, LANES), lambda b, h, i: (b, h, i, 0)),
                  pl.BlockSpec((1, 2, n_all, LANES), lambda b, h, i: (b, h, 0, 0)),
                  pl.BlockSpec((1, 1, LANES, n_all), lambda b, h, i: (b, h, 0, 0))],
        out_specs=pl.BlockSpec((1, tq, LANES), lambda b, h, i: (b, i, h)),
        compiler_params=_cparams(("arbitrary", "arbitrary", "arbitrary")),
        name="attn_m",
    )(qm, km, vmt)


def _attn_diff_kernel(lq1_ref, lk1_ref, lq2_ref, lk2_ref, g_sub_ref, q_ref, k_ref, vt_ref, o_ref):
    lam = (jnp.exp(jnp.sum(lq1_ref[...] * lk1_ref[...], axis=-1, keepdims=True))
           - jnp.exp(jnp.sum(lq2_ref[...] * lk2_ref[...], axis=-1, keepdims=True)) + LAM_INIT)
    tq = q_ref.shape[3]
    o = _attend_t(q_ref[0, 0].reshape(2 * tq, LANES), k_ref[0, 0], vt_ref[0, 0])
    ot = o[:, :tq] - lam * o[:, tq:]
    ot = ot * lax.rsqrt(jnp.mean(ot * ot, axis=0, keepdims=True) + NORM_EPS) * g_sub_ref[...]
    o_ref[0] = (ot * (1.0 - LAM_INIT)).T.astype(BF16)


def _attn_diff(lams, g_sub_col, qd, k12, vdt, tq):
    n_batch, heads, _, n_lat, _ = qd.shape
    n_all = k12.shape[2]

    def small(a):
        return pl.BlockSpec(a.shape, lambda b, h, i: (0,) * a.ndim)

    return pl.pallas_call(
        _attn_diff_kernel,
        out_shape=jax.ShapeDtypeStruct((n_batch, n_lat, heads * LANES), BF16),
        grid=(n_batch, heads, n_lat // tq),
        in_specs=[small(lams[0]), small(lams[1]), small(lams[2]), small(lams[3]), small(g_sub_col),
                  pl.BlockSpec((1, 1, 2, tq, LANES), lambda b, h, i: (b, h, 0, i, 0)),
                  pl.BlockSpec((1, 1, n_all, LANES), lambda b, h, i: (b, h, 0, 0)),
                  pl.BlockSpec((1, 1, LANES, n_all), lambda b, h, i: (b, h, 0, 0))],
        out_specs=pl.BlockSpec((1, tq, LANES), lambda b, h, i: (b, i, h)),
        compiler_params=_cparams(("arbitrary", "arbitrary", "arbitrary")),
        name="attn_d",
    )(*lams, g_sub_col, qd, k12, vdt)


def _post_kernel(d_model, tiles_per_batch, om_ref, od_ref, x_ref, mod_ref, w_out_ref, g_ffn_ref, ws13_ref,
                 ws2_ref, wr_ref, br_ref,
                 xp_ref, h2p_ref, eidx_ref, rank_ref, wts_ref, cnt_ref, carry_ref):
    i = pl.program_id(0)
    tm = x_ref.shape[0]
    n_exp = wr_ref.shape[0]
    per_group = n_exp // N_GROUPS

    @pl.when(i == 0)
    def _():
        carry_ref[...] = jnp.zeros_like(carry_ref)

    b = i // tiles_per_batch

    def modv(j):
        return mod_ref[pl.ds(b, 1), j * d_model:(j + 1) * d_model]

    half = om_ref.shape[1]
    y = _dot(om_ref[...], w_out_ref[0:half, :]) + _dot(od_ref[...], w_out_ref[half:, :])
    x1 = x_ref[...] + modv(2) * y
    h2 = _rms_rows(x1, g_ffn_ref[...]) * (1.0 + modv(4)) + modv(3)
    h2p_ref[...] = _pack_pair(h2[:, :d_model // 2], h2[:, d_model // 2:])

    h2b = h2.astype(BF16)
    gu = _dot(h2b, ws13_ref[...])
    ff = gu.shape[1] // 2
    g = gu[:, :ff]
    act = g * jax.nn.sigmoid(g) * gu[:, ff:]
    shared = _dot(act.astype(BF16), ws2_ref[...])
    xp_ref[...] = x1 + modv(5) * shared

    logits = lax.dot_general(wr_ref[...], h2, (((1,), (1,)), ((), ())), preferred_element_type=F32,
                             precision=lax.Precision.HIGHEST)
    s = jax.nn.sigmoid(logits)
    ssel = s + br_ref[...]
    s3 = ssel.reshape(N_GROUPS, per_group, tm)
    m1 = jnp.max(s3, axis=1, keepdims=True)
    eq = s3 == m1
    n_eq = jnp.sum(jnp.where(eq, 1.0, 0.0), axis=1, keepdims=True)
    m2 = jnp.max(jnp.where(eq, NEG_INF, s3), axis=1, keepdims=True)
    grp = m1 + jnp.where(n_eq >= 2.0, m1, m2)

    gi = lax.broadcasted_iota(jnp.int32, grp.shape, 0).astype(F32)
    gcur = grp
    gsel = jnp.zeros_like(grp)
    for _ in range(TOPK_GROUPS):
        gm = jnp.max(gcur, axis=0, keepdims=True)
        first = jnp.min(jnp.where(gcur == gm, gi, float(N_GROUPS)), axis=0, keepdims=True)
        oh = gi == first
        gsel = jnp.where(oh, 1.0, gsel)
        gcur = jnp.where(oh, NEG_INF, gcur)
    cur = jnp.where(gsel > 0.0, s3, NEG_INF).reshape(n_exp, tm)

    ie = lax.broadcasted_iota(jnp.int32, (n_exp, tm), 0).astype(F32)
    sel = jnp.zeros((n_exp, tm), F32)
    e_rows = []
    s_rows = []
    for _ in range(TOP_K):
        m = jnp.max(cur, axis=0, keepdims=True)
        first = jnp.min(jnp.where(cur == m, ie, float(n_exp)), axis=0, keepdims=True)
        oh = ie == first
        cur = jnp.where(oh, NEG_INF, cur)
        sel = jnp.where(oh, 1.0, sel)
        e_rows.append(first)
        s_rows.append(jnp.sum(jnp.where(oh, s, 0.0), axis=0, keepdims=True))
    s_tot = s_rows[0]
    for r in s_rows[1:]:
        s_tot = s_tot + r

    tr = lax.broadcasted_iota(jnp.int32, (tm, tm), 0)
    tc = lax.broadcasted_iota(jnp.int32, (tm, tm), 1)
    upper = jnp.where(tr < tc, 1.0, 0.0).astype(BF16)
    rank = _dot(sel.astype(BF16), upper) + carry_ref[...]
    for k in range(TOP_K):
        eidx_ref[k:k + 1, :] = e_rows[k].astype(jnp.int32)
        rank_ref[k:k + 1, :] = jnp.sum(jnp.where(ie == e_rows[k], rank, 0.0), axis=0,
                                       keepdims=True).astype(jnp.int32)
        wts_ref[k:k + 1, :] = s_rows[k] / s_tot * ROUTED_SCALE
    carry_ref[...] = carry_ref[...] + jnp.sum(sel, axis=1, keepdims=True)
    cnt_ref[...] = carry_ref[...].astype(jnp.int32)


def _post(om, od, x2d, mod, w_out, g_ffn, ws13, ws2, wr_t, br, n_lat, tm):
    t, d = x2d.shape
    n_exp = wr_t.shape[0]

    def full(a):
        return pl.BlockSpec(a.shape, lambda i: (0,) * a.ndim)

    row = lambda w: pl.BlockSpec((tm, w), lambda i: (i, 0))
    col = pl.BlockSpec((TOP_K, tm), lambda i: (0, i))
    return pl.pallas_call(
        functools.partial(_post_kernel, d, n_lat // tm),
        out_shape=(jax.ShapeDtypeStruct((t, d), F32), jax.ShapeDtypeStruct((t, d // 2), jnp.uint32),
                   jax.ShapeDtypeStruct((TOP_K, t), jnp.int32), jax.ShapeDtypeStruct((TOP_K, t), jnp.int32),
                   jax.ShapeDtypeStruct((TOP_K, t), F32), jax.ShapeDtypeStruct((n_exp, 1), jnp.int32)),
        grid=(t // tm,),
        in_specs=[row(om.shape[1]), row(od.shape[1]), row(d), full(mod), full(w_out), full(g_ffn),
                  full(ws13), full(ws2), full(wr_t), full(br)],
        out_specs=(row(d), row(d // 2), col, col, col, pl.BlockSpec((n_exp, 1), lambda i: (0, 0))),
        scratch_shapes=[pltpu.VMEM((n_exp, 1), F32)],
        compiler_params=_cparams(("arbitrary",)),
        name="post",
    )(om, od, x2d, mod, w_out, g_ffn, ws13, ws2, wr_t, br)


_HI_MASK = 0xFFFF0000


def _pack_pair(a, b):
    lo = lax.bitcast_convert_type(a.astype(BF16).astype(F32), jnp.uint32) >> 16
    hi = lax.bitcast_convert_type(b.astype(BF16).astype(F32), jnp.uint32) & jnp.uint32(_HI_MASK)
    return lo | hi


def _unpack_pair(w):
    return (lax.bitcast_convert_type(w << 16, F32),
            lax.bitcast_convert_type(w & jnp.uint32(_HI_MASK), F32))


def _row_copy(src, src_row, dst, dst_row, sem):
    return pltpu.make_async_copy(src.at[pl.ds(src_row, 1), :], dst.at[pl.ds(dst_row, 1), :], sem)


def _slots_kernel(start_ref, eidx_ref, rank_ref, dest_ref):
    e = eidx_ref[...]

    def body(x, acc):
        return jnp.where(e == x, start_ref[x], acc)

    dest_ref[...] = lax.fori_loop(0, start_ref.shape[0], body, jnp.zeros_like(e), unroll=8) + rank_ref[...]


def _slots(start, eidx, rank):
    k, t = eidx.shape
    tt = min(t, 2048)
    spec = pl.BlockSpec((k, tt), lambda i, *_: (0, i))
    return pl.pallas_call(
        _slots_kernel,
        out_shape=jax.ShapeDtypeStruct((k, t), jnp.int32),
        grid_spec=pltpu.PrefetchScalarGridSpec(num_scalar_prefetch=1, grid=(t // tt,), in_specs=[spec, spec],
                                               out_specs=spec),
        compiler_params=_cparams(("arbitrary",)),
        name="slots",
    )(start, eidx, rank)


def _dest_copy(t, td, dest_hbm, d_s, sem):
    return pltpu.make_async_copy(dest_hbm.at[:, pl.ds(t * td, td)], d_s.at[t % 2], sem.at[t % 2])


def _dispatch_kernel(h2p_hbm, dest_hbm, xs_hbm, hbuf, d_s, fsem, rsem, dsem):
    i = pl.program_id(0)
    n = pl.num_programs(0)
    td = hbuf.shape[1]

    def fetch(t):
        slot = t % 3
        return pltpu.make_async_copy(h2p_hbm.at[pl.ds(t * td, td), :], hbuf.at[slot], fsem.at[slot])

    def wait_rows(t):
        for _ in range(td * TOP_K):
            _row_copy(hbuf.at[0], 0, xs_hbm, 0, dsem.at[t % 2]).wait()

    @pl.when(i == 0)
    def _():
        fetch(0).start()
        _dest_copy(0, td, dest_hbm, d_s, rsem).start()

        @pl.when(n > 1)
        def _():
            fetch(1).start()

    @pl.when(i + 1 < n)
    def _():
        _dest_copy(i + 1, td, dest_hbm, d_s, rsem).start()

    fetch(i).wait()
    _dest_copy(i, td, dest_hbm, d_s, rsem).wait()

    src = hbuf.at[i % 3]
    dst = d_s.at[i % 2]
    sem = dsem.at[i % 2]
    for j in range(td):
        for k in range(TOP_K):
            _row_copy(src, j, xs_hbm, dst[k, j], sem).start()

    @pl.when(i >= 1)
    def _():
        wait_rows(i - 1)

    @pl.when(i + 2 < n)
    def _():
        fetch(i + 2).start()

    @pl.when(i == n - 1)
    def _():
        wait_rows(i)


def _dispatch(h2p, dest, td):
    t, dw = h2p.shape
    any_spec = pl.BlockSpec(memory_space=pl.ANY)
    return pl.pallas_call(
        _dispatch_kernel,
        out_shape=jax.ShapeDtypeStruct((t * TOP_K, dw), jnp.uint32),
        grid=(t // td,),
        in_specs=[any_spec, any_spec],
        out_specs=any_spec,
        scratch_shapes=[pltpu.VMEM((3, td, dw), jnp.uint32), pltpu.SMEM((2, TOP_K, td), jnp.int32),
                        pltpu.SemaphoreType.DMA((3,)), pltpu.SemaphoreType.DMA((2,)),
                        pltpu.SemaphoreType.DMA((2,))],
        compiler_params=_cparams(("arbitrary",)),
        name="dispatch",
    )(h2p, dest)


def _experts_kernel(blk_ref, e_ref, lo_ref, hi_ref, xs_ref, w1_ref, w3_ref, w2_ref, ys_ref, w13_b, w2_b):
    j = pl.program_id(0)
    lo = lo_ref[j]
    hi = hi_ref[j]
    bm, dw = xs_ref.shape
    ff = w1_ref.shape[2]
    base = blk_ref[j] * bm

    @pl.when(jnp.logical_or(j == 0, e_ref[j] != e_ref[jnp.maximum(j - 1, 0)]))
    def _():
        w13_b[:, :ff] = w1_ref[0].astype(BF16)
        w13_b[:, ff:] = w3_ref[0].astype(BF16)
        w2_b[...] = w2_ref[0].astype(BF16)

    def swiglu():
        x_lo, x_hi = _unpack_pair(xs_ref[...])
        gu = _dot(x_lo.astype(BF16), w13_b[:dw, :]) + _dot(x_hi.astype(BF16), w13_b[dw:, :])
        g = gu[:, :ff]
        act = g * jax.nn.sigmoid(g) * gu[:, ff:]
        y = _dot(act.astype(BF16), w2_b[...])
        return _pack_pair(y[:, :dw], y[:, dw:])

    @pl.when(jnp.logical_and(hi > lo, lo == base))
    def _():
        ys_ref[...] = swiglu()

    @pl.when(jnp.logical_and(hi > lo, lo != base))
    def _():
        rows = base + lax.broadcasted_iota(jnp.int32, (bm, 1), 0)
        ys_ref[...] = jnp.where(rows >= lo, swiglu(), ys_ref[...])


def _experts(item_blk, item_e, item_lo, item_hi, xs, w1, w3, w2, bm):
    n_slots, dw = xs.shape
    _, d, ff = w1.shape
    return pl.pallas_call(
        _experts_kernel,
        out_shape=jax.ShapeDtypeStruct((n_slots, dw), jnp.uint32),
        grid_spec=pltpu.PrefetchScalarGridSpec(
            num_scalar_prefetch=4,
            grid=(item_blk.shape[0],),
            in_specs=[pl.BlockSpec((bm, dw), lambda j, blk, e, lo, hi: (blk[j], 0)),
                      pl.BlockSpec((1, d, ff), lambda j, blk, e, lo, hi: (e[j], 0, 0)),
                      pl.BlockSpec((1, d, ff), lambda j, blk, e, lo, hi: (e[j], 0, 0)),
                      pl.BlockSpec((1, ff, d), lambda j, blk, e, lo, hi: (e[j], 0, 0))],
            out_specs=pl.BlockSpec((bm, dw), lambda j, blk, e, lo, hi: (blk[j], 0)),
            scratch_shapes=[pltpu.VMEM((d, 2 * ff), BF16), pltpu.VMEM((ff, d), BF16)]),
        compiler_params=_cparams(("arbitrary",)),
        name="experts",
    )(item_blk, item_e, item_lo, item_hi, xs, w1, w3, w2)


def _combine_kernel(d_model, tiles_per_batch, xp_ref, wts_ref, mod_ref, g_final_ref, dest_hbm, ys_hbm, o_ref,
                    gbuf, d_s, rsem, gsem):
    s = pl.program_id(0)
    n = pl.num_programs(0) - 1
    tc = xp_ref.shape[0]
    dw = d_model // 2

    @pl.when(s == 0)
    def _():
        _dest_copy(0, tc, dest_hbm, d_s, rsem).start()

    @pl.when(s + 1 < n)
    def _():
        _dest_copy(s + 1, tc, dest_hbm, d_s, rsem).start()

    @pl.when(s < n)
    def _():
        _dest_copy(s, tc, dest_hbm, d_s, rsem).wait()
        src = d_s.at[s % 2]
        dst = gbuf.at[s % 2]
        sem = gsem.at[s % 2]
        for j in range(tc):
            for k in range(TOP_K):
                _row_copy(ys_hbm, src[k, j], dst.at[k], j, sem).start()

    @pl.when(s >= 1)
    def _():
        buf = (s - 1) % 2
        for _ in range(tc * TOP_K):
            _row_copy(ys_hbm, 0, gbuf.at[0, 0], 0, gsem.at[buf]).wait()
        w = wts_ref[...]
        r_lo = r_hi = None
        for k in range(TOP_K):
            y_lo, y_hi = _unpack_pair(gbuf[buf, k])
            wk = w[:, k:k + 1]
            r_lo = wk * y_lo if r_lo is None else r_lo + wk * y_lo
            r_hi = wk * y_hi if r_hi is None else r_hi + wk * y_hi
        b = (s - 1) // tiles_per_batch
        gt2 = mod_ref[pl.ds(b, 1), 5 * d_model:6 * d_model]
        v_lo = xp_ref[:, :dw] + gt2[:, :dw] * r_lo
        v_hi = xp_ref[:, dw:] + gt2[:, dw:] * r_hi
        ms = (jnp.sum(v_lo * v_lo, axis=-1, keepdims=True)
              + jnp.sum(v_hi * v_hi, axis=-1, keepdims=True)) / d_model
        inv = lax.rsqrt(ms + NORM_EPS)
        o_ref[:, :dw] = v_lo * inv * g_final_ref[:, :dw]
        o_ref[:, dw:] = v_hi * inv * g_final_ref[:, dw:]


def _combine(xp, wts_t, mod, g_final, dest, ys, n_lat, tc):
    t, d = xp.shape
    any_spec = pl.BlockSpec(memory_space=pl.ANY)
    prev = lambda s: (jnp.maximum(s - 1, 0), 0)
    return pl.pallas_call(
        functools.partial(_combine_kernel, d, n_lat // tc),
        out_shape=jax.ShapeDtypeStruct((t, d), F32),
        grid=(t // tc + 1,),
        in_specs=[pl.BlockSpec((tc, d), prev), pl.BlockSpec((tc, TOP_K), prev),
                  pl.BlockSpec(mod.shape, lambda s: (0, 0)), pl.BlockSpec(g_final.shape, lambda s: (0, 0)),
                  any_spec, any_spec],
        out_specs=pl.BlockSpec((tc, d), prev),
        scratch_shapes=[pltpu.VMEM((2, TOP_K, tc, d // 2), jnp.uint32), pltpu.SMEM((2, TOP_K, tc), jnp.int32),
                        pltpu.SemaphoreType.DMA((2,)), pltpu.SemaphoreType.DMA((2,))],
        compiler_params=_cparams(("arbitrary",)),
        name="combine",
    )(xp, wts_t, mod, g_final, dest, ys)


def _swap_halves(w, width):
    r, c = w.shape
    return w.reshape(r, c // width, 2, width // 2)[:, :, ::-1, :].reshape(r, c)


def _pad_cols(w, left, total):
    return jnp.pad(w, ((0, 0), (left, total - left - w.shape[1])))


def _layout_weights(w_in, w_uq, w_ukv):
    s0 = MLA_Q_RANK
    s1 = s0 + MLA_KV_RANK
    s2 = s1 + MLA_ROPE
    s3 = s2 + DIFF_HEADS * 2 * DIFF_QK
    s4 = s3 + DIFF_HEADS * 2 * DIFF_QK
    w_kr, w_dq, w_dk = w_in[:, s1:s2], w_in[:, s2:s3], w_in[:, s3:s4]
    w_in_ext = jnp.concatenate([
        w_in[:, :s1], w_dq, _swap_halves(w_dq, DIFF_QK), w_dk, _swap_halves(w_dk, DIFF_QK), w_in[:, s4:],
        _pad_cols(w_kr, MLA_NOPE, LANES), _pad_cols(_swap_halves(w_kr, MLA_ROPE), MLA_NOPE, LANES)],
        axis=1).astype(BF16)

    qk = MLA_NOPE + MLA_ROPE
    qa, qb, kk, vv = [], [], [], []
    for h in range(MLA_HEADS):
        wq = w_uq[:, h * qk:(h + 1) * qk]
        qa.append(_pad_cols(wq, 0, LANES))
        qb.append(_pad_cols(_swap_halves(wq[:, MLA_NOPE:], MLA_ROPE), MLA_NOPE, LANES))
        wkv = w_ukv[:, h * (MLA_NOPE + MLA_V):(h + 1) * (MLA_NOPE + MLA_V)]
        kk.append(_pad_cols(wkv[:, :MLA_NOPE], 0, LANES))
        vv.append(wkv[:, MLA_NOPE:])
    w_uq_ext = jnp.concatenate(qa + qb, axis=1).astype(BF16)
    w_ukv_ext = jnp.concatenate(kk + vv, axis=1).astype(BF16)
    return w_in_ext, w_uq_ext, w_ukv_ext


def _rotary_tables(n_ctx, n_lat):
    def angles(rot_dim):
        n_freq = rot_dim // 4
        inv = ROPE_BASE ** (-(jnp.arange(n_freq, dtype=F32) / n_freq))
        rows = n_lat // GRID_W
        row = jnp.repeat(jnp.arange(rows, dtype=F32), GRID_W)
        col = jnp.tile(jnp.arange(GRID_W, dtype=F32), rows)
        theta = jnp.concatenate([row[:, None] * inv, col[:, None] * inv], axis=-1)
        theta = jnp.concatenate([jnp.zeros((n_ctx, 2 * n_freq), F32), theta], axis=0)
        return jnp.cos(theta), jnp.sin(theta)

    n = n_ctx + n_lat
    cm, sm = angles(MLA_ROPE)
    cd, sd = angles(DIFF_QK)
    zeros = lambda w: jnp.zeros((n, w), F32)
    ones = lambda w: jnp.ones((n, w), F32)
    pad_m = LANES - MLA_NOPE - MLA_ROPE
    log2e = 1.0 / math.log(2.0)
    scale_m = log2e / math.sqrt(MLA_NOPE + MLA_ROPE)
    scale_d = log2e / math.sqrt(DIFF_QK)
    cq = jnp.concatenate([ones(MLA_NOPE), cm, cm, zeros(pad_m)], axis=1)
    sq = jnp.concatenate([zeros(MLA_NOPE), -sm, sm, zeros(pad_m)], axis=1)
    ck = jnp.concatenate([zeros(MLA_NOPE), cm, cm, zeros(pad_m)], axis=1)
    cdd = jnp.concatenate([cd, cd, cd, cd], axis=1)
    sdd = jnp.concatenate([-sd, sd, -sd, sd], axis=1)
    lo = jnp.concatenate([ones(DIFF_QK), zeros(DIFF_QK)], axis=1)
    hi = 1.0 - lo
    return jnp.concatenate([cq * scale_m, sq * scale_m, ck, sq,
                            cdd * lo * scale_d, sdd * lo * scale_d, cdd * hi * scale_d, sdd * hi * scale_d,
                            cdd, sdd], axis=1)


def kernel(x, c, ctx, c_ctx, w_mod, b_mod, g_attn, g_ffn, w_in, g_q_lat, w_uq, g_kv_lat, w_ukv, lam_q1, lam_k1,
           lam_q2, lam_k2, g_subln, w_out, w_router, router_bias, w1, w3, w2, ws1, ws3, ws2, g_final):
    n_batch, n_lat, d = x.shape
    n_ctx = ctx.shape[1]
    t = n_batch * n_lat
    tm = 256
    tq = 512
    bm = 256
    td = 128
    assert w_mod.shape[0] == 1 and n_ctx % tm == 0 and n_lat % tq == 0 and n_lat % tm == 0 and t % td == 0
    assert (t * TOP_K) % bm == 0

    cc = jnp.concatenate([c, c_ctx[None, :], jnp.zeros((8 - n_batch - 1, d), F32)], axis=0)
    mod = _modulation(cc, w_mod[0], b_mod)

    w_in_ext, w_uq_ext, w_ukv_ext = _layout_weights(w_in[0], w_uq[0], w_ukv[0])
    tables = _rotary_tables(n_ctx, n_lat)
    qm, km, vmt, qd, k12, vdt = _projections(ctx, x, mod, g_attn, w_in_ext, g_q_lat, w_uq_ext, g_kv_lat,
                                             w_ukv_ext, tables, tm)
    om = _attn_mla(qm, km, vmt, tq)
    od = _attn_diff((lam_q1, lam_k1, lam_q2, lam_k2), g_subln.reshape(-1, 1), qd, k12, vdt, tq // 2)

    ws13 = jnp.concatenate([ws1[0], ws3[0]], axis=1).astype(BF16)
    xp, h2p, eidx, rank, wts, cnt = _post(
        om.reshape(t, -1), od.reshape(t, -1), x.reshape(t, d), mod, w_out[0].astype(BF16), g_ffn, ws13,
        ws2[0].astype(BF16), w_router[0].T, router_bias[0][:, None], n_lat, tm)

    counts = cnt[:, 0]
    start = (jnp.cumsum(counts) - counts).astype(jnp.int32)
    n_slots = t * TOP_K
    n_blocks = n_slots // bm
    blk_b = jnp.arange(n_blocks, dtype=jnp.int32) * bm
    exp_b = jnp.concatenate([start[1:], jnp.full((1,), n_slots, jnp.int32)])
    pos_blk = jnp.arange(n_blocks, dtype=jnp.int32) + jnp.sum(exp_b[None, :] < blk_b[:, None], axis=1)
    pos_exp = jnp.arange(N_EXPERTS, dtype=jnp.int32) + jnp.sum(blk_b[None, :] <= exp_b[:, None], axis=1)
    slot = jnp.arange(n_blocks + N_EXPERTS, dtype=jnp.int32)[:, None]
    bounds = (jnp.sum(jnp.where(pos_blk[None, :] == slot, blk_b[None, :], 0), axis=1)
              + jnp.sum(jnp.where(pos_exp[None, :] == slot, exp_b[None, :], 0), axis=1))
    item_lo, item_hi = bounds[:-1], bounds[1:]
    item_blk = jnp.minimum(item_lo // bm, n_blocks - 1)
    item_e = jnp.clip(jnp.sum(start[None, :] <= item_lo[:, None], axis=1) - 1, 0, N_EXPERTS - 1).astype(jnp.int32)

    dest = _slots(start, eidx, rank)
    xs = _dispatch(h2p, dest, td)
    ys = _experts(item_blk, item_e, item_lo, item_hi, xs, w1[0], w3[0], w2[0], bm)
    out = _combine(xp, wts.T, mod, g_final[None, :], dest, ys, n_lat, td)
    return out.reshape(n_batch, n_lat, d)
```

```python
import functools
import math

import jax
import jax.numpy as jnp
from jax import lax
from jax.experimental import pallas as pl
from jax.experimental.pallas import tpu as pltpu

GRID_W = 64
ROPE_BASE = 10000.0
NORM_EPS = 1e-6
MLA_HEADS = 8
MLA_NOPE = 64
MLA_ROPE = 32
MLA_V = 64
MLA_Q_RANK = 256
MLA_KV_RANK = 128
DIFF_HEADS = 4
DIFF_QK = 64
DIFF_V = 2 * DIFF_QK
N_EXPERTS = 256
TOP_K = 8
N_GROUPS = 8
TOPK_GROUPS = 4
ROUTED_SCALE = 2.5
LAM_INIT = 0.8 - 0.6 * math.exp(-0.3 * 0)

LANES = 128
VMEM_LIMIT = 48 * 1024 * 1024

F32 = jnp.float32
BF16 = jnp.bfloat16
NEG_INF = float("-inf")


def _cparams(sem):
    return pltpu.CompilerParams(dimension_semantics=sem, vmem_limit_bytes=VMEM_LIMIT)


def _rms_rows(x, g):
    return x * lax.rsqrt(jnp.mean(x * x, axis=-1, keepdims=True) + NORM_EPS) * g


def _dot(a, b):
    return jnp.dot(a, b, preferred_element_type=F32)


def _dot_nt(a, b):
    return lax.dot_general(a, b, (((1,), (1,)), ((), ())), preferred_element_type=F32)


def _mod_kernel(c_ref, w_ref, b_ref, o_ref):
    a = c_ref[...]
    a = a * jax.nn.sigmoid(a)
    o_ref[...] = jnp.dot(a, w_ref[...], preferred_element_type=F32,
                         precision=lax.Precision.HIGHEST) + b_ref[...]


def _modulation(cc, w_mod, b_mod):
    rows, d = cc.shape
    cols = w_mod.shape[1]
    tn = 1536
    return pl.pallas_call(
        _mod_kernel,
        out_shape=jax.ShapeDtypeStruct((rows, cols), F32),
        grid=(cols // tn,),
        in_specs=[pl.BlockSpec((rows, d), lambda j: (0, 0)),
                  pl.BlockSpec((d, tn), lambda j: (0, j)),
                  pl.BlockSpec((1, tn), lambda j: (0, j))],
        out_specs=pl.BlockSpec((rows, tn), lambda j: (0, j)),
        compiler_params=_cparams(("arbitrary",)),
        name="mod",
    )(cc, w_mod, b_mod)


_C_Q = 0
_C_KV = _C_Q + MLA_Q_RANK
_C_DQ = _C_KV + MLA_KV_RANK
_C_DQS = _C_DQ + DIFF_HEADS * 2 * DIFF_QK
_C_DK = _C_DQS + DIFF_HEADS * 2 * DIFF_QK
_C_DKS = _C_DK + DIFF_HEADS * 2 * DIFF_QK
_C_DV = _C_DKS + DIFF_HEADS * 2 * DIFF_QK
_C_KRA = _C_DV + DIFF_HEADS * DIFF_V
_C_KRB = _C_KRA + LANES
_IN_EXT = _C_KRB + LANES
_N_TAB = 4


def _proj_kernel(n_batch, d_model, ctx_ref, x_ref, mod_ref, g_attn_ref, w_in_ref, g_q_ref, w_uq_ref,
                 g_kv_ref, w_ukv_ref, tab_ref,
                 qm_ref, km_ref, vmt_ref, qd_ref, k12_ref, vdt_ref, *, n_ctx_tiles):
    b = pl.program_id(0)
    i = pl.program_id(1)
    is_ctx = i < n_ctx_tiles
    xin = jnp.where(is_ctx, ctx_ref[0], x_ref[0])
    row = jnp.where(is_ctx, n_batch, b)
    sh1 = mod_ref[pl.ds(row, 1), 0:d_model]
    sc1 = mod_ref[pl.ds(row, 1), d_model:2 * d_model]
    h = _rms_rows(xin, g_attn_ref[...]) * (1.0 + sc1) + sh1
    p = _dot(h.astype(BF16), w_in_ref[...])

    cq = _rms_rows(p[:, _C_Q:_C_Q + MLA_Q_RANK], g_q_ref[...])
    qa = _dot(cq.astype(BF16), w_uq_ref[...])
    ckv = _rms_rows(p[:, _C_KV:_C_KV + MLA_KV_RANK], g_kv_ref[...])
    kv = _dot(ckv.astype(BF16), w_ukv_ref[...])

    def tab(j):
        return tab_ref[:, j * LANES:(j + 1) * LANES]

    kr = p[:, _C_KRA:_C_KRA + LANES] * tab(0) + p[:, _C_KRB:_C_KRB + LANES] * tab(1)
    first_half = lax.broadcasted_iota(jnp.int32, (1, LANES), 1) < DIFF_QK
    hw = MLA_HEADS * LANES
    for hd in range(MLA_HEADS):
        lo = hd * LANES
        qm_ref[0, hd] = (qa[:, lo:lo + LANES] * tab(0) + qa[:, hw + lo:hw + lo + LANES] * tab(1)).astype(BF16)
        km_ref[0, hd] = (kv[:, lo:lo + LANES] + kr).astype(BF16)
    for pr in range(MLA_HEADS // 2):
        lo = hw + pr * LANES
        vmt_ref[0, pr] = kv[:, lo:lo + LANES].T.astype(BF16)
    for hd in range(DIFF_HEADS):
        lo = hd * LANES
        dq = p[:, _C_DQ + lo:_C_DQ + lo + LANES]
        dqs = p[:, _C_DQS + lo:_C_DQS + lo + LANES]
        qr = dq * tab(2) + dqs * tab(3)
        qd_ref[0, hd, 0] = jnp.where(first_half, qr, 0.0).astype(BF16)
        qd_ref[0, hd, 1] = jnp.where(first_half, 0.0, qr).astype(BF16)
        dk = p[:, _C_DK + lo:_C_DK + lo + LANES]
        dks = p[:, _C_DKS + lo:_C_DKS + lo + LANES]
        k12_ref[0, hd] = (dk * tab(2) + dks * tab(3)).astype(BF16)
        vdt_ref[0, hd] = p[:, _C_DV + lo:_C_DV + lo + LANES].T.astype(BF16)


def _projections(ctx, x, mod, g_attn, w_in_ext, g_q, w_uq_ext, g_kv, w_ukv_ext, tables, tm):
    n_batch, n_ctx, d = ctx.shape
    n_lat = x.shape[1]
    n_all = n_ctx + n_lat
    nct = n_ctx // tm
    nt = n_all // tm

    def full(a):
        return pl.BlockSpec(a.shape, lambda b, i: (0,) * a.ndim)

    def k_spec(heads):
        return pl.BlockSpec((1, heads, tm, LANES), lambda b, i: (b, 0, i, 0))

    def vt_spec(heads):
        return pl.BlockSpec((1, heads, LANES, tm), lambda b, i: (b, 0, 0, i))

    def lat(i):
        return jnp.maximum(i - nct, 0)

    def sds(*shape):
        return jax.ShapeDtypeStruct((n_batch,) + shape, BF16)

    return pl.pallas_call(
        functools.partial(_proj_kernel, n_batch, d, n_ctx_tiles=nct),
        out_shape=(sds(MLA_HEADS, n_lat, LANES), sds(MLA_HEADS, n_all, LANES), sds(MLA_HEADS // 2, LANES, n_all),
                   sds(DIFF_HEADS, 2, n_lat, LANES), sds(DIFF_HEADS, n_all, LANES),
                   sds(DIFF_HEADS, LANES, n_all)),
        grid=(n_batch, nt),
        in_specs=[pl.BlockSpec((1, tm, d), lambda b, i: (b, jnp.minimum(i, nct - 1), 0)),
                  pl.BlockSpec((1, tm, d), lambda b, i: (b, jnp.maximum(i - nct, 0), 0)),
                  full(mod), full(g_attn), full(w_in_ext), full(g_q), full(w_uq_ext), full(g_kv),
                  full(w_ukv_ext),
                  pl.BlockSpec((tm, _N_TAB * LANES), lambda b, i: (i, 0))],
        out_specs=(pl.BlockSpec((1, MLA_HEADS, tm, LANES), lambda b, i: (b, 0, lat(i), 0)),
                   k_spec(MLA_HEADS), vt_spec(MLA_HEADS // 2),
                   pl.BlockSpec((1, DIFF_HEADS, 2, tm, LANES), lambda b, i: (b, 0, 0, lat(i), 0)),
                   k_spec(DIFF_HEADS), vt_spec(DIFF_HEADS)),
        compiler_params=_cparams(("arbitrary", "arbitrary")),
        name="proj",
    )(ctx, x, mod, g_attn, w_in_ext, g_q, w_uq_ext, g_kv, w_ukv_ext, tables)


def _attend_t(q, k, vt):
    st = _dot_nt(k, q)
    et = jnp.exp2(st - jnp.max(st, axis=0, keepdims=True))
    l = jnp.sum(et, axis=0, keepdims=True)
    return _dot(vt, et.astype(BF16)) * (1.0 / l)


def _attn_mla_kernel(q_ref, k_ref, vt_ref, o_ref):
    sts = [_dot_nt(k_ref[0, j], q_ref[0, j]) for j in range(2)]
    ets = [jnp.exp2(st - jnp.max(st, axis=0, keepdims=True)) for st in sts]
    ls = [jnp.sum(et, axis=0, keepdims=True) for et in ets]
    outs = [_dot(vt_ref[0, 0, j * MLA_V:(j + 1) * MLA_V, :], ets[j].astype(BF16)) * (1.0 / ls[j]) for j in range(2)]
    o_ref[0] = jnp.concatenate(outs, axis=0).T.astype(BF16)


def _attn_mla(qm, km, vmt, tq):
    n_batch, heads, n_lat, _ = qm.shape
    n_all = km.shape[2]
    return pl.pallas_call(
        _attn_mla_kernel,
        out_shape=jax.ShapeDtypeStruct((n_batch, n_lat, heads // 2 * LANES), BF16),
        grid=(n_batch, heads // 2, n_lat // tq),
        in_specs=[pl.BlockSpec((1, 2, tq, LANES), lambda b, h, i: (b, h, i, 0)),
                  pl.BlockSpec((1, 2, n_all, LANES), lambda b, h, i: (b, h, 0, 0)),
                  pl.BlockSpec((1, 1, LANES, n_all), lambda b, h, i: (b, h, 0, 0))],
        out_specs=pl.BlockSpec((1, tq, LANES), lambda b, h, i: (b, i, h)),
        compiler_params=_cparams(("arbitrary", "arbitrary", "arbitrary")),
        name="attn_m",
    )(qm, km, vmt)


def _attn_diff_kernel(lq1_ref, lk1_ref, lq2_ref, lk2_ref, g_sub_ref, q_ref, k_ref, vt_ref, o_ref):
    lam = (jnp.exp(jnp.sum(lq1_ref[...] * lk1_ref[...], axis=-1, keepdims=True))
           - jnp.exp(jnp.sum(lq2_ref[...] * lk2_ref[...], axis=-1, keepdims=True)) + LAM_INIT)
    tq = q_ref.shape[3]
    o = _attend_t(q_ref[0, 0].reshape(2 * tq, LANES), k_ref[0, 0], vt_ref[0, 0])
    ot = o[:, :tq] - lam * o[:, tq:]
    ot = ot * lax.rsqrt(jnp.mean(ot * ot, axis=0, keepdims=True) + NORM_EPS) * g_sub_ref[...]
    o_ref[0] = (ot * (1.0 - LAM_INIT)).T.astype(BF16)


def _attn_diff(lams, g_sub_col, qd, k12, vdt, tq):
    n_batch, heads, _, n_lat, _ = qd.shape
    n_all = k12.shape[2]

    def small(a):
        return pl.BlockSpec(a.shape, lambda b, h, i: (0,) * a.ndim)

    return pl.pallas_call(
        _attn_diff_kernel,
        out_shape=jax.ShapeDtypeStruct((n_batch, n_lat, heads * LANES), BF16),
        grid=(n_batch, heads, n_lat // tq),
        in_specs=[small(lams[0]), small(lams[1]), small(lams[2]), small(lams[3]), small(g_sub_col),
                  pl.BlockSpec((1, 1, 2, tq, LANES), lambda b, h, i: (b, h, 0, i, 0)),
                  pl.BlockSpec((1, 1, n_all, LANES), lambda b, h, i: (b, h, 0, 0)),
                  pl.BlockSpec((1, 1, LANES, n_all), lambda b, h, i: (b, h, 0, 0))],
        out_specs=pl.BlockSpec((1, tq, LANES), lambda b, h, i: (b, i, h)),
        compiler_params=_cparams(("arbitrary", "arbitrary", "arbitrary")),
        name="attn_d",
    )(*lams, g_sub_col, qd, k12, vdt)


def _post_kernel(d_model, tiles_per_batch, om_ref, od_ref, x_ref, mod_ref, w_out_ref, g_ffn_ref, ws13_ref,
                 ws2_ref, wr_hi_ref, wr_lo_ref, br_ref,
                 xp_ref, h2p_ref, eidx_ref, rank_ref, wts_ref, cnt_ref, carry_ref):
    i = pl.program_id(0)
    tm = x_ref.shape[0]
    n_exp = wr_hi_ref.shape[0]
    per_group = n_exp // N_GROUPS

    @pl.when(i == 0)
    def _():
        carry_ref[...] = jnp.zeros_like(carry_ref)

    b = i // tiles_per_batch

    def modv(j):
        return mod_ref[pl.ds(b, 1), j * d_model:(j + 1) * d_model]

    half = om_ref.shape[1]
    y = _dot(om_ref[...], w_out_ref[0:half, :]) + _dot(od_ref[...], w_out_ref[half:, :])
    x1 = x_ref[...] + modv(2) * y
    h2 = _rms_rows(x1, g_ffn_ref[...]) * (1.0 + modv(4)) + modv(3)
    h2p_ref[...] = _pack_pair(h2[:, :d_model // 2], h2[:, d_model // 2:])

    h2b = h2.astype(BF16)
    gu = _dot(h2b, ws13_ref[...])
    ff = gu.shape[1] // 2
    g = gu[:, :ff]
    act = g * jax.nn.sigmoid(g) * gu[:, ff:]
    shared = _dot(act.astype(BF16), ws2_ref[...])
    xp_ref[...] = x1 + modv(5) * shared

    h2_lo = (h2 - h2b.astype(F32)).astype(BF16)
    logits = _dot_nt(wr_hi_ref[...], h2b) + (_dot_nt(wr_hi_ref[...], h2_lo)
                                             + _dot_nt(wr_lo_ref[...], h2b))
    s = jax.nn.sigmoid(logits)
    ssel = s + br_ref[...]
    s3 = ssel.reshape(N_GROUPS, per_group, tm)
    m1 = jnp.max(s3, axis=1, keepdims=True)
    eq = s3 == m1
    n_eq = jnp.sum(jnp.where(eq, 1.0, 0.0), axis=1, keepdims=True)
    m2 = jnp.max(jnp.where(eq, NEG_INF, s3), axis=1, keepdims=True)
    grp = m1 + jnp.where(n_eq >= 2.0, m1, m2)

    gi = lax.broadcasted_iota(jnp.int32, grp.shape, 0).astype(F32)
    gcur = grp
    gsel = jnp.zeros_like(grp)
    for _ in range(TOPK_GROUPS):
        gm = jnp.max(gcur, axis=0, keepdims=True)
        first = jnp.min(jnp.where(gcur == gm, gi, float(N_GROUPS)), axis=0, keepdims=True)
        oh = gi == first
        gsel = jnp.where(oh, 1.0, gsel)
        gcur = jnp.where(oh, NEG_INF, gcur)
    cur = jnp.where(gsel > 0.0, s3, NEG_INF).reshape(n_exp, tm)

    ie = lax.broadcasted_iota(jnp.int32, (n_exp, tm), 0).astype(F32)
    sel = jnp.zeros((n_exp, tm), F32)
    e_rows = []
    s_rows = []
    for _ in range(TOP_K):
        m = jnp.max(cur, axis=0, keepdims=True)
        first = jnp.min(jnp.where(cur == m, ie, float(n_exp)), axis=0, keepdims=True)
        oh = ie == first
        cur = jnp.where(oh, NEG_INF, cur)
        sel = jnp.where(oh, 1.0, sel)
        e_rows.append(first)
        s_rows.append(jnp.sum(jnp.where(oh, s, 0.0), axis=0, keepdims=True))
    s_tot = s_rows[0]
    for r in s_rows[1:]:
        s_tot = s_tot + r

    tr = lax.broadcasted_iota(jnp.int32, (tm, tm), 0)
    tc = lax.broadcasted_iota(jnp.int32, (tm, tm), 1)
    upper = jnp.where(tr < tc, 1.0, 0.0).astype(BF16)
    rank = _dot(sel.astype(BF16), upper) + carry_ref[...]
    for k in range(TOP_K):
        eidx_ref[k:k + 1, :] = e_rows[k].astype(jnp.int32)
        rank_ref[k:k + 1, :] = jnp.sum(jnp.where(ie == e_rows[k], rank, 0.0), axis=0,
                                       keepdims=True).astype(jnp.int32)
        wts_ref[k:k + 1, :] = s_rows[k] / s_tot * ROUTED_SCALE
    carry_ref[...] = carry_ref[...] + jnp.sum(sel, axis=1, keepdims=True)
    cnt_ref[...] = carry_ref[...].astype(jnp.int32)


def _post(om, od, x2d, mod, w_out, g_ffn, ws13, ws2, wr_t, br, n_lat, tm):
    t, d = x2d.shape
    n_exp = wr_t.shape[0]
    wr_hi = wr_t.astype(BF16)
    wr_lo = (wr_t - wr_hi.astype(F32)).astype(BF16)

    def full(a):
        return pl.BlockSpec(a.shape, lambda i: (0,) * a.ndim)

    row = lambda w: pl.BlockSpec((tm, w), lambda i: (i, 0))
    col = pl.BlockSpec((TOP_K, tm), lambda i: (0, i))
    return pl.pallas_call(
        functools.partial(_post_kernel, d, n_lat // tm),
        out_shape=(jax.ShapeDtypeStruct((t, d), F32), jax.ShapeDtypeStruct((t, d // 2), jnp.uint32),
                   jax.ShapeDtypeStruct((TOP_K, t), jnp.int32), jax.ShapeDtypeStruct((TOP_K, t), jnp.int32),
                   jax.ShapeDtypeStruct((TOP_K, t), F32), jax.ShapeDtypeStruct((n_exp, 1), jnp.int32)),
        grid=(t // tm,),
        in_specs=[row(om.shape[1]), row(od.shape[1]), row(d), full(mod), full(w_out), full(g_ffn),
                  full(ws13), full(ws2), full(wr_hi), full(wr_lo), full(br)],
        out_specs=(row(d), row(d // 2), col, col, col, pl.BlockSpec((n_exp, 1), lambda i: (0, 0))),
        scratch_shapes=[pltpu.VMEM((n_exp, 1), F32)],
        compiler_params=_cparams(("arbitrary",)),
        name="post",
    )(om, od, x2d, mod, w_out, g_ffn, ws13, ws2, wr_hi, wr_lo, br)


_HI_MASK = 0xFFFF0000


def _pack_pair(a, b):
    lo = lax.bitcast_convert_type(a.astype(BF16).astype(F32), jnp.uint32) >> 16
    hi = lax.bitcast_convert_type(b.astype(BF16).astype(F32), jnp.uint32) & jnp.uint32(_HI_MASK)
    return lo | hi


def _unpack_pair(w):
    return (lax.bitcast_convert_type(w << 16, F32),
            lax.bitcast_convert_type(w & jnp.uint32(_HI_MASK), F32))


def _row_copy(src, src_row, dst, dst_row, sem):
    return pltpu.make_async_copy(src.at[pl.ds(src_row, 1), :], dst.at[pl.ds(dst_row, 1), :], sem)


def _slots_kernel(start_ref, eidx_ref, rank_ref, dest_ref):
    e = eidx_ref[...]

    def body(x, acc):
        return jnp.where(e == x, start_ref[x], acc)

    dest_ref[...] = lax.fori_loop(0, start_ref.shape[0], body, jnp.zeros_like(e), unroll=8) + rank_ref[...]


def _slots(start, eidx, rank):
    k, t = eidx.shape
    tt = min(t, 2048)
    spec = pl.BlockSpec((k, tt), lambda i, *_: (0, i))
    return pl.pallas_call(
        _slots_kernel,
        out_shape=jax.ShapeDtypeStruct((k, t), jnp.int32),
        grid_spec=pltpu.PrefetchScalarGridSpec(num_scalar_prefetch=1, grid=(t // tt,), in_specs=[spec, spec],
                                               out_specs=spec),
        compiler_params=_cparams(("arbitrary",)),
        name="slots",
    )(start, eidx, rank)


def _dest_copy(t, td, dest_hbm, d_s, sem):
    return pltpu.make_async_copy(dest_hbm.at[:, pl.ds(t * td, td)], d_s.at[t % 2], sem.at[t % 2])


def _dispatch_kernel(h2p_hbm, dest_hbm, xs_hbm, hbuf, d_s, fsem, rsem, dsem):
    i = pl.program_id(0)
    n = pl.num_programs(0)
    td = hbuf.shape[1]

    def fetch(t):
        slot = t % 3
        return pltpu.make_async_copy(h2p_hbm.at[pl.ds(t * td, td), :], hbuf.at[slot], fsem.at[slot])

    def wait_rows(t):
        for _ in range(td * TOP_K):
            _row_copy(hbuf.at[0], 0, xs_hbm, 0, dsem.at[t % 2]).wait()

    @pl.when(i == 0)
    def _():
        fetch(0).start()
        _dest_copy(0, td, dest_hbm, d_s, rsem).start()

        @pl.when(n > 1)
        def _():
            fetch(1).start()

    @pl.when(i + 1 < n)
    def _():
        _dest_copy(i + 1, td, dest_hbm, d_s, rsem).start()

    fetch(i).wait()
    _dest_copy(i, td, dest_hbm, d_s, rsem).wait()

    src = hbuf.at[i % 3]
    dst = d_s.at[i % 2]
    sem = dsem.at[i % 2]
    for j in range(td):
        for k in range(TOP_K):
            _row_copy(src, j, xs_hbm, dst[k, j], sem).start()

    @pl.when(i >= 1)
    def _():
        wait_rows(i - 1)

    @pl.when(i + 2 < n)
    def _():
        fetch(i + 2).start()

    @pl.when(i == n - 1)
    def _():
        wait_rows(i)


def _dispatch(h2p, dest, td):
    t, dw = h2p.shape
    any_spec = pl.BlockSpec(memory_space=pl.ANY)
    return pl.pallas_call(
        _dispatch_kernel,
        out_shape=jax.ShapeDtypeStruct((t * TOP_K, dw), jnp.uint32),
        grid=(t // td,),
        in_specs=[any_spec, any_spec],
        out_specs=any_spec,
        scratch_shapes=[pltpu.VMEM((3, td, dw), jnp.uint32), pltpu.SMEM((2, TOP_K, td), jnp.int32),
                        pltpu.SemaphoreType.DMA((3,)), pltpu.SemaphoreType.DMA((2,)),
                        pltpu.SemaphoreType.DMA((2,))],
        compiler_params=_cparams(("arbitrary",)),
        name="dispatch",
    )(h2p, dest)


def _experts_kernel(blk_ref, e_ref, lo_ref, hi_ref, slot_ref, next_ref, xs_ref, w1_hbm, w3_hbm, w2_hbm, ys_ref,
                    w1_f, w3_f, w2_f, w13_b, w2_b, wsem):
    j = pl.program_id(0)
    lo = lo_ref[j]
    hi = hi_ref[j]
    bm, dw = xs_ref.shape
    ff = w1_hbm.shape[2]
    base = blk_ref[j] * bm

    def fetch(e, slot):
        return (pltpu.make_async_copy(w1_hbm.at[e], w1_f.at[slot], wsem.at[slot, 0]),
                pltpu.make_async_copy(w3_hbm.at[e], w3_f.at[slot], wsem.at[slot, 1]),
                pltpu.make_async_copy(w2_hbm.at[e], w2_f.at[slot], wsem.at[slot, 2]))

    @pl.when(j == 0)
    def _():
        for cp in fetch(e_ref[0], 0):
            cp.start()

    @pl.when(jnp.logical_or(j == 0, e_ref[j] != e_ref[jnp.maximum(j - 1, 0)]))
    def _():
        slot = slot_ref[j]
        for cp in fetch(e_ref[j], slot):
            cp.wait()

        @pl.when(next_ref[j] >= 0)
        def _():
            for cp in fetch(next_ref[j], 1 - slot):
                cp.start()

        w13_b[:, :ff] = w1_f[slot].astype(BF16)
        w13_b[:, ff:] = w3_f[slot].astype(BF16)
        w2_b[...] = w2_f[slot].astype(BF16)

    def swiglu():
        x_lo, x_hi = _unpack_pair(xs_ref[...])
        gu = _dot(x_lo.astype(BF16), w13_b[:dw, :]) + _dot(x_hi.astype(BF16), w13_b[dw:, :])
        g = gu[:, :ff]
        act = g * jax.nn.sigmoid(g) * gu[:, ff:]
        y = _dot(act.astype(BF16), w2_b[...])
        return _pack_pair(y[:, :dw], y[:, dw:])

    @pl.when(jnp.logical_and(hi > lo, lo == base))
    def _():
        ys_ref[...] = swiglu()

    @pl.when(jnp.logical_and(hi > lo, lo != base))
    def _():
        rows = base + lax.broadcasted_iota(jnp.int32, (bm, 1), 0)
        ys_ref[...] = jnp.where(rows >= lo, swiglu(), ys_ref[...])


def _experts(items, xs, w1, w3, w2, bm):
    n_slots, dw = xs.shape
    _, d, ff = w1.shape
    any_spec = pl.BlockSpec(memory_space=pl.ANY)
    rows = pl.BlockSpec((bm, dw), lambda j, blk, *_: (blk[j], 0))
    return pl.pallas_call(
        _experts_kernel,
        out_shape=jax.ShapeDtypeStruct((n_slots, dw), jnp.uint32),
        grid_spec=pltpu.PrefetchScalarGridSpec(
            num_scalar_prefetch=len(items),
            grid=(items[0].shape[0],),
            in_specs=[rows, any_spec, any_spec, any_spec],
            out_specs=rows,
            scratch_shapes=[pltpu.VMEM((2, d, ff), F32), pltpu.VMEM((2, d, ff), F32), pltpu.VMEM((2, ff, d), F32),
                            pltpu.VMEM((d, 2 * ff), BF16), pltpu.VMEM((ff, d), BF16),
                            pltpu.SemaphoreType.DMA((2, 3))]),
        compiler_params=_cparams(("arbitrary",)),
        name="experts",
    )(*items, xs, w1, w3, w2)


def _combine_kernel(d_model, tiles_per_batch, xp_ref, wts_ref, mod_ref, g_final_ref, dest_hbm, ys_hbm, o_ref,
                    gbuf, d_s, rsem, gsem):
    s = pl.program_id(0)
    n = pl.num_programs(0) - 1
    tc = xp_ref.shape[0]
    dw = d_model // 2

    @pl.when(s == 0)
    def _():
        _dest_copy(0, tc, dest_hbm, d_s, rsem).start()

    @pl.when(s + 1 < n)
    def _():
        _dest_copy(s + 1, tc, dest_hbm, d_s, rsem).start()

    @pl.when(s < n)
    def _():
        _dest_copy(s, tc, dest_hbm, d_s, rsem).wait()
        src = d_s.at[s % 2]
        dst = gbuf.at[s % 2]
        sem = gsem.at[s % 2]
        for j in range(tc):
            for k in range(TOP_K):
                _row_copy(ys_hbm, src[k, j], dst.at[k], j, sem).start()

    @pl.when(s >= 1)
    def _():
        buf = (s - 1) % 2
        for _ in range(tc * TOP_K):
            _row_copy(ys_hbm, 0, gbuf.at[0, 0], 0, gsem.at[buf]).wait()
        w = wts_ref[...]
        r_lo = r_hi = None
        for k in range(TOP_K):
            y_lo, y_hi = _unpack_pair(gbuf[buf, k])
            wk = w[:, k:k + 1]
            r_lo = wk * y_lo if r_lo is None else r_lo + wk * y_lo
            r_hi = wk * y_hi if r_hi is None else r_hi + wk * y_hi
        b = (s - 1) // tiles_per_batch
        gt2 = mod_ref[pl.ds(b, 1), 5 * d_model:6 * d_model]
        v_lo = xp_ref[:, :dw] + gt2[:, :dw] * r_lo
        v_hi = xp_ref[:, dw:] + gt2[:, dw:] * r_hi
        ms = (jnp.sum(v_lo * v_lo, axis=-1, keepdims=True)
              + jnp.sum(v_hi * v_hi, axis=-1, keepdims=True)) / d_model
        inv = lax.rsqrt(ms + NORM_EPS)
        o_ref[:, :dw] = v_lo * inv * g_final_ref[:, :dw]
        o_ref[:, dw:] = v_hi * inv * g_final_ref[:, dw:]


def _combine(xp, wts_t, mod, g_final, dest, ys, n_lat, tc):
    t, d = xp.shape
    any_spec = pl.BlockSpec(memory_space=pl.ANY)
    prev = lambda s: (jnp.maximum(s - 1, 0), 0)
    return pl.pallas_call(
        functools.partial(_combine_kernel, d, n_lat // tc),
        out_shape=jax.ShapeDtypeStruct((t, d), F32),
        grid=(t // tc + 1,),
        in_specs=[pl.BlockSpec((tc, d), prev), pl.BlockSpec((tc, TOP_K), prev),
                  pl.BlockSpec(mod.shape, lambda s: (0, 0)), pl.BlockSpec(g_final.shape, lambda s: (0, 0)),
                  any_spec, any_spec],
        out_specs=pl.BlockSpec((tc, d), prev),
        scratch_shapes=[pltpu.VMEM((2, TOP_K, tc, d // 2), jnp.uint32), pltpu.SMEM((2, TOP_K, tc), jnp.int32),
                        pltpu.SemaphoreType.DMA((2,)), pltpu.SemaphoreType.DMA((2,))],
        compiler_params=_cparams(("arbitrary",)),
        name="combine",
    )(xp, wts_t, mod, g_final, dest, ys)


def _swap_halves(w, width):
    r, c = w.shape
    return w.reshape(r, c // width, 2, width // 2)[:, :, ::-1, :].reshape(r, c)


def _pad_cols(w, left, total):
    return jnp.pad(w, ((0, 0), (left, total - left - w.shape[1])))


def _layout_weights(w_in, w_uq, w_ukv):
    s0 = MLA_Q_RANK
    s1 = s0 + MLA_KV_RANK
    s2 = s1 + MLA_ROPE
    s3 = s2 + DIFF_HEADS * 2 * DIFF_QK
    s4 = s3 + DIFF_HEADS * 2 * DIFF_QK
    log2e = 1.0 / math.log(2.0)
    w_kr, w_dq, w_dk = w_in[:, s1:s2], w_in[:, s2:s3] * (log2e / math.sqrt(DIFF_QK)), w_in[:, s3:s4]
    w_uq = w_uq * (log2e / math.sqrt(MLA_NOPE + MLA_ROPE))
    w_in_ext = jnp.concatenate([
        w_in[:, :s1], w_dq, _swap_halves(w_dq, DIFF_QK), w_dk, _swap_halves(w_dk, DIFF_QK), w_in[:, s4:],
        _pad_cols(w_kr, MLA_NOPE, LANES), _pad_cols(_swap_halves(w_kr, MLA_ROPE), MLA_NOPE, LANES)],
        axis=1).astype(BF16)

    qk = MLA_NOPE + MLA_ROPE
    qa, qb, kk, vv = [], [], [], []
    for h in range(MLA_HEADS):
        wq = w_uq[:, h * qk:(h + 1) * qk]
        qa.append(_pad_cols(wq, 0, LANES))
        qb.append(_pad_cols(_swap_halves(wq[:, MLA_NOPE:], MLA_ROPE), MLA_NOPE, LANES))
        wkv = w_ukv[:, h * (MLA_NOPE + MLA_V):(h + 1) * (MLA_NOPE + MLA_V)]
        kk.append(_pad_cols(wkv[:, :MLA_NOPE], 0, LANES))
        vv.append(wkv[:, MLA_NOPE:])
    w_uq_ext = jnp.concatenate(qa + qb, axis=1).astype(BF16)
    w_ukv_ext = jnp.concatenate(kk + vv, axis=1).astype(BF16)
    return w_in_ext, w_uq_ext, w_ukv_ext


def _rotary_tables(n_ctx, n_lat):
    def angles(rot_dim):
        n_freq = rot_dim // 4
        inv = ROPE_BASE ** (-(jnp.arange(n_freq, dtype=F32) / n_freq))
        rows = n_lat // GRID_W
        row = jnp.repeat(jnp.arange(rows, dtype=F32), GRID_W)
        col = jnp.tile(jnp.arange(GRID_W, dtype=F32), rows)
        theta = jnp.concatenate([row[:, None] * inv, col[:, None] * inv], axis=-1)
        theta = jnp.concatenate([jnp.zeros((n_ctx, 2 * n_freq), F32), theta], axis=0)
        return jnp.cos(theta), jnp.sin(theta)

    n = n_ctx + n_lat
    cm, sm = angles(MLA_ROPE)
    cd, sd = angles(DIFF_QK)
    zeros = lambda w: jnp.zeros((n, w), F32)
    ones = lambda w: jnp.ones((n, w), F32)
    pad_m = LANES - MLA_NOPE - MLA_ROPE
    cq = jnp.concatenate([ones(MLA_NOPE), cm, cm, zeros(pad_m)], axis=1)
    sq = jnp.concatenate([zeros(MLA_NOPE), -sm, sm, zeros(pad_m)], axis=1)
    cdd = jnp.concatenate([cd, cd, cd, cd], axis=1)
    sdd = jnp.concatenate([-sd, sd, -sd, sd], axis=1)
    return jnp.concatenate([cq, sq, cdd, sdd], axis=1)


def kernel(x, c, ctx, c_ctx, w_mod, b_mod, g_attn, g_ffn, w_in, g_q_lat, w_uq, g_kv_lat, w_ukv, lam_q1, lam_k1,
           lam_q2, lam_k2, g_subln, w_out, w_router, router_bias, w1, w3, w2, ws1, ws3, ws2, g_final):
    n_batch, n_lat, d = x.shape
    n_ctx = ctx.shape[1]
    t = n_batch * n_lat
    tm = 256
    tq = 512
    bm = 256
    td = 128
    assert w_mod.shape[0] == 1 and n_ctx % tm == 0 and n_lat % tq == 0 and n_lat % tm == 0 and t % td == 0
    assert (t * TOP_K) % bm == 0

    cc = jnp.concatenate([c, c_ctx[None, :], jnp.zeros((8 - n_batch - 1, d), F32)], axis=0)
    mod = _modulation(cc, w_mod[0], b_mod)

    w_in_ext, w_uq_ext, w_ukv_ext = _layout_weights(w_in[0], w_uq[0], w_ukv[0])
    tables = _rotary_tables(n_ctx, n_lat)
    qm, km, vmt, qd, k12, vdt = _projections(ctx, x, mod, g_attn, w_in_ext, g_q_lat, w_uq_ext, g_kv_lat,
                                             w_ukv_ext, tables, tm)
    om = _attn_mla(qm, km, vmt, tq)
    od = _attn_diff((lam_q1, lam_k1, lam_q2, lam_k2), g_subln.reshape(-1, 1), qd, k12, vdt, tq)

    ws13 = jnp.concatenate([ws1[0], ws3[0]], axis=1).astype(BF16)
    xp, h2p, eidx, rank, wts, cnt = _post(
        om.reshape(t, -1), od.reshape(t, -1), x.reshape(t, d), mod, w_out[0].astype(BF16), g_ffn, ws13,
        ws2[0].astype(BF16), w_router[0].T, router_bias[0][:, None], n_lat, tm)

    counts = cnt[:, 0]
    start = (jnp.cumsum(counts) - counts).astype(jnp.int32)
    n_slots = t * TOP_K
    n_blocks = n_slots // bm
    blk_b = jnp.arange(n_blocks, dtype=jnp.int32) * bm
    exp_b = jnp.concatenate([start[1:], jnp.full((1,), n_slots, jnp.int32)])
    pos_blk = jnp.arange(n_blocks, dtype=jnp.int32) + jnp.sum(exp_b[None, :] < blk_b[:, None], axis=1)
    pos_exp = jnp.arange(N_EXPERTS, dtype=jnp.int32) + jnp.sum(blk_b[None, :] <= exp_b[:, None], axis=1)
    slot = jnp.arange(n_blocks + N_EXPERTS, dtype=jnp.int32)[:, None]
    bounds = (jnp.sum(jnp.where(pos_blk[None, :] == slot, blk_b[None, :], 0), axis=1)
              + jnp.sum(jnp.where(pos_exp[None, :] == slot, exp_b[None, :], 0), axis=1))
    item_lo, item_hi = bounds[:-1], bounds[1:]
    item_blk = jnp.minimum(item_lo // bm, n_blocks - 1)
    item_e = jnp.clip(jnp.sum(start[None, :] <= item_lo[:, None], axis=1) - 1, 0, N_EXPERTS - 1).astype(jnp.int32)
    changed = jnp.concatenate([jnp.zeros((1,), jnp.int32), (item_e[1:] != item_e[:-1]).astype(jnp.int32)])
    item_slot = jnp.cumsum(changed).astype(jnp.int32) % 2
    later = jnp.where(item_e[None, :] > item_e[:, None], item_e[None, :], N_EXPERTS)
    item_next = jnp.min(later, axis=1)
    item_next = jnp.where(item_next == N_EXPERTS, -1, item_next).astype(jnp.int32)

    dest = _slots(start, eidx, rank)
    xs = _dispatch(h2p, dest, td)
    ys = _experts((item_blk, item_e, item_lo, item_hi, item_slot, item_next), xs, w1[0], w3[0], w2[0], bm)
    out = _combine(xp, wts.T, mod, g_final[None, :], dest, ys, n_lat, td)
    return out.reshape(n_batch, n_lat, d)
```

```python
import functools
import math

import jax
import jax.numpy as jnp
import numpy as np
from jax import lax
from jax.experimental import pallas as pl
from jax.experimental.pallas import tpu as pltpu

GRID_W = 64
ROPE_BASE = 10000.0
NORM_EPS = 1e-6
MLA_HEADS = 8
MLA_NOPE = 64
MLA_ROPE = 32
MLA_V = 64
MLA_Q_RANK = 256
MLA_KV_RANK = 128
DIFF_HEADS = 4
DIFF_QK = 64
DIFF_V = 2 * DIFF_QK
N_EXPERTS = 256
TOP_K = 8
N_GROUPS = 8
TOPK_GROUPS = 4
ROUTED_SCALE = 2.5
LAM_INIT = 0.8 - 0.6 * math.exp(-0.3 * 0)

LANES = 128
VMEM_LIMIT = 48 * 1024 * 1024

F32 = jnp.float32
BF16 = jnp.bfloat16
NEG_INF = float("-inf")


def _cparams(sem):
    return pltpu.CompilerParams(dimension_semantics=sem, vmem_limit_bytes=VMEM_LIMIT)


def _rms_rows(x, g):
    return x * lax.rsqrt(jnp.mean(x * x, axis=-1, keepdims=True) + NORM_EPS) * g


def _dot(a, b):
    return jnp.dot(a, b, preferred_element_type=F32)


def _dot_nt(a, b):
    return lax.dot_general(a, b, (((1,), (1,)), ((), ())), preferred_element_type=F32)


def _mod_kernel(c_ref, w_ref, b_ref, o_ref):
    a = c_ref[...]
    a = a * jax.nn.sigmoid(a)
    o_ref[...] = jnp.dot(a, w_ref[...], preferred_element_type=F32,
                         precision=lax.Precision.HIGHEST) + b_ref[...]


def _modulation(cc, w_mod, b_mod):
    rows, d = cc.shape
    cols = w_mod.shape[1]
    tn = 1536
    return pl.pallas_call(
        _mod_kernel,
        out_shape=jax.ShapeDtypeStruct((rows, cols), F32),
        grid=(cols // tn,),
        in_specs=[pl.BlockSpec((rows, d), lambda j: (0, 0)),
                  pl.BlockSpec((d, tn), lambda j: (0, j)),
                  pl.BlockSpec((1, tn), lambda j: (0, j))],
        out_specs=pl.BlockSpec((rows, tn), lambda j: (0, j)),
        compiler_params=_cparams(("arbitrary",)),
        name="mod",
    )(cc, w_mod, b_mod)


_C_Q = 0
_C_KV = _C_Q + MLA_Q_RANK
_C_DQ = _C_KV + MLA_KV_RANK
_C_DQS = _C_DQ + DIFF_HEADS * 2 * DIFF_QK
_C_DK = _C_DQS + DIFF_HEADS * 2 * DIFF_QK
_C_DKS = _C_DK + DIFF_HEADS * 2 * DIFF_QK
_C_DV = _C_DKS + DIFF_HEADS * 2 * DIFF_QK
_C_KRA = _C_DV + DIFF_HEADS * DIFF_V
_C_KRB = _C_KRA + LANES
_IN_EXT = _C_KRB + LANES
_N_TAB = 4


def _proj_kernel(n_batch, d_model, ctx_ref, x_ref, mod_ref, g_attn_ref, w_in_ref, g_q_ref, w_uq_ref,
                 g_kv_ref, w_ukv_ref, tab_ref,
                 qm_ref, km_ref, vmt_ref, qd_ref, k12_ref, vdt_ref, *, n_ctx_tiles):
    b = pl.program_id(0)
    i = pl.program_id(1)
    is_ctx = i < n_ctx_tiles
    xin = jnp.where(is_ctx, ctx_ref[0], x_ref[0])
    row = jnp.where(is_ctx, n_batch, b)
    sh1 = mod_ref[pl.ds(row, 1), 0:d_model]
    sc1 = mod_ref[pl.ds(row, 1), d_model:2 * d_model]
    h = _rms_rows(xin, g_attn_ref[...]) * (1.0 + sc1) + sh1
    p = _dot(h.astype(BF16), w_in_ref[...])

    cq = _rms_rows(p[:, _C_Q:_C_Q + MLA_Q_RANK], g_q_ref[...])
    qa = _dot(cq.astype(BF16), w_uq_ref[...])
    ckv = _rms_rows(p[:, _C_KV:_C_KV + MLA_KV_RANK], g_kv_ref[...])
    kv = _dot(ckv.astype(BF16), w_ukv_ref[...])

    def tab(j):
        return tab_ref[:, j * LANES:(j + 1) * LANES]

    kr = p[:, _C_KRA:_C_KRA + LANES] * tab(0) + p[:, _C_KRB:_C_KRB + LANES] * tab(1)
    first_half = lax.broadcasted_iota(jnp.int32, (1, LANES), 1) < DIFF_QK
    hw = MLA_HEADS * LANES
    for hd in range(MLA_HEADS):
        lo = hd * LANES
        qm_ref[0, hd] = (qa[:, lo:lo + LANES] * tab(0) + qa[:, hw + lo:hw + lo + LANES] * tab(1)).astype(BF16)
        km_ref[0, hd] = (kv[:, lo:lo + LANES] + kr).astype(BF16)
    for pr in range(MLA_HEADS // 2):
        lo = hw + pr * LANES
        vmt_ref[0, pr] = kv[:, lo:lo + LANES].T.astype(BF16)
    for hd in range(DIFF_HEADS):
        lo = hd * LANES
        dq = p[:, _C_DQ + lo:_C_DQ + lo + LANES]
        dqs = p[:, _C_DQS + lo:_C_DQS + lo + LANES]
        qr = dq * tab(2) + dqs * tab(3)
        qd_ref[0, hd, 0] = jnp.where(first_half, qr, 0.0).astype(BF16)
        qd_ref[0, hd, 1] = jnp.where(first_half, 0.0, qr).astype(BF16)
        dk = p[:, _C_DK + lo:_C_DK + lo + LANES]
        dks = p[:, _C_DKS + lo:_C_DKS + lo + LANES]
        k12_ref[0, hd] = (dk * tab(2) + dks * tab(3)).astype(BF16)
        vdt_ref[0, hd] = p[:, _C_DV + lo:_C_DV + lo + LANES].T.astype(BF16)


def _projections(ctx, x, mod, g_attn, w_in_ext, g_q, w_uq_ext, g_kv, w_ukv_ext, tables, tm):
    n_batch, n_ctx, d = ctx.shape
    n_lat = x.shape[1]
    n_all = n_ctx + n_lat
    nct = n_ctx // tm
    nt = n_all // tm

    def full(a):
        return pl.BlockSpec(a.shape, lambda b, i: (0,) * a.ndim)

    def k_spec(heads):
        return pl.BlockSpec((1, heads, tm, LANES), lambda b, i: (b, 0, i, 0))

    def vt_spec(heads):
        return pl.BlockSpec((1, heads, LANES, tm), lambda b, i: (b, 0, 0, i))

    def lat(i):
        return jnp.maximum(i - nct, 0)

    def sds(*shape):
        return jax.ShapeDtypeStruct((n_batch,) + shape, BF16)

    return pl.pallas_call(
        functools.partial(_proj_kernel, n_batch, d, n_ctx_tiles=nct),
        out_shape=(sds(MLA_HEADS, n_lat, LANES), sds(MLA_HEADS, n_all, LANES), sds(MLA_HEADS // 2, LANES, n_all),
                   sds(DIFF_HEADS, 2, n_lat, LANES), sds(DIFF_HEADS, n_all, LANES),
                   sds(DIFF_HEADS, LANES, n_all)),
        grid=(n_batch, nt),
        in_specs=[pl.BlockSpec((1, tm, d), lambda b, i: (b, jnp.minimum(i, nct - 1), 0)),
                  pl.BlockSpec((1, tm, d), lambda b, i: (b, jnp.maximum(i - nct, 0), 0)),
                  full(mod), full(g_attn), full(w_in_ext), full(g_q), full(w_uq_ext), full(g_kv),
                  full(w_ukv_ext),
                  pl.BlockSpec((tm, _N_TAB * LANES), lambda b, i: (i, 0))],
        out_specs=(pl.BlockSpec((1, MLA_HEADS, tm, LANES), lambda b, i: (b, 0, lat(i), 0)),
                   k_spec(MLA_HEADS), vt_spec(MLA_HEADS // 2),
                   pl.BlockSpec((1, DIFF_HEADS, 2, tm, LANES), lambda b, i: (b, 0, 0, lat(i), 0)),
                   k_spec(DIFF_HEADS), vt_spec(DIFF_HEADS)),
        compiler_params=_cparams(("arbitrary", "arbitrary")),
        name="proj",
    )(ctx, x, mod, g_attn, w_in_ext, g_q, w_uq_ext, g_kv, w_ukv_ext, tables)


def _attend_t(q, k, vt):
    st = _dot_nt(k, q)
    et = jnp.exp2(st - jnp.max(st, axis=0, keepdims=True))
    l = jnp.sum(et, axis=0, keepdims=True)
    return _dot(vt, et.astype(BF16)) * (1.0 / l)


def _attn_mla_kernel(q_ref, k_ref, vt_ref, o_ref):
    sts = [_dot_nt(k_ref[0, j], q_ref[0, j]) for j in range(2)]
    ets = [jnp.exp2(st - jnp.max(st, axis=0, keepdims=True)) for st in sts]
    ls = [jnp.sum(et, axis=0, keepdims=True) for et in ets]
    outs = [_dot(vt_ref[0, 0, j * MLA_V:(j + 1) * MLA_V, :], ets[j].astype(BF16)) * (1.0 / ls[j]) for j in range(2)]
    o_ref[0] = jnp.concatenate(outs, axis=0).T.astype(BF16)


def _attn_mla(qm, km, vmt, tq):
    n_batch, heads, n_lat, _ = qm.shape
    n_all = km.shape[2]
    return pl.pallas_call(
        _attn_mla_kernel,
        out_shape=jax.ShapeDtypeStruct((n_batch, n_lat, heads // 2 * LANES), BF16),
        grid=(n_batch, heads // 2, n_lat // tq),
        in_specs=[pl.BlockSpec((1, 2, tq, LANES), lambda b, h, i: (b, h, i, 0)),
                  pl.BlockSpec((1, 2, n_all, LANES), lambda b, h, i: (b, h, 0, 0)),
                  pl.BlockSpec((1, 1, LANES, n_all), lambda b, h, i: (b, h, 0, 0))],
        out_specs=pl.BlockSpec((1, tq, LANES), lambda b, h, i: (b, i, h)),
        compiler_params=_cparams(("arbitrary", "arbitrary", "arbitrary")),
        name="attn_m",
    )(qm, km, vmt)


def _attn_diff_kernel(lq1_ref, lk1_ref, lq2_ref, lk2_ref, g_sub_ref, q_ref, k_ref, vt_ref, o_ref):
    lam = (jnp.exp(jnp.sum(lq1_ref[...] * lk1_ref[...], axis=-1, keepdims=True))
           - jnp.exp(jnp.sum(lq2_ref[...] * lk2_ref[...], axis=-1, keepdims=True)) + LAM_INIT)
    tq = q_ref.shape[3]
    o = _attend_t(q_ref[0, 0].reshape(2 * tq, LANES), k_ref[0, 0], vt_ref[0, 0])
    ot = o[:, :tq] - lam * o[:, tq:]
    ot = ot * lax.rsqrt(jnp.mean(ot * ot, axis=0, keepdims=True) + NORM_EPS) * g_sub_ref[...]
    o_ref[0] = (ot * (1.0 - LAM_INIT)).T.astype(BF16)


def _attn_diff(lams, g_sub_col, qd, k12, vdt, tq):
    n_batch, heads, _, n_lat, _ = qd.shape
    n_all = k12.shape[2]

    def small(a):
        return pl.BlockSpec(a.shape, lambda b, h, i: (0,) * a.ndim)

    return pl.pallas_call(
        _attn_diff_kernel,
        out_shape=jax.ShapeDtypeStruct((n_batch, n_lat, heads * LANES), BF16),
        grid=(n_batch, heads, n_lat // tq),
        in_specs=[small(lams[0]), small(lams[1]), small(lams[2]), small(lams[3]), small(g_sub_col),
                  pl.BlockSpec((1, 1, 2, tq, LANES), lambda b, h, i: (b, h, 0, i, 0)),
                  pl.BlockSpec((1, 1, n_all, LANES), lambda b, h, i: (b, h, 0, 0)),
                  pl.BlockSpec((1, 1, LANES, n_all), lambda b, h, i: (b, h, 0, 0))],
        out_specs=pl.BlockSpec((1, tq, LANES), lambda b, h, i: (b, i, h)),
        compiler_params=_cparams(("arbitrary", "arbitrary", "arbitrary")),
        name="attn_d",
    )(*lams, g_sub_col, qd, k12, vdt)


def _post_kernel(d_model, tiles_per_batch, om_ref, od_ref, x_ref, mod_ref, w_out_ref, g_ffn_ref, ws13_ref,
                 ws2_ref, wr_hi_ref, wr_lo_ref, br_ref,
                 xp_ref, h2p_ref, eidx_ref, rank_ref, wts_ref, cnt_ref, carry_ref):
    i = pl.program_id(0)
    tm = x_ref.shape[0]
    n_exp = wr_hi_ref.shape[0]
    per_group = n_exp // N_GROUPS

    @pl.when(i == 0)
    def _():
        carry_ref[...] = jnp.zeros_like(carry_ref)

    b = i // tiles_per_batch

    def modv(j):
        return mod_ref[pl.ds(b, 1), j * d_model:(j + 1) * d_model]

    half = om_ref.shape[1]
    y = _dot(om_ref[...], w_out_ref[0:half, :]) + _dot(od_ref[...], w_out_ref[half:, :])
    x1 = x_ref[...] + modv(2) * y
    h2 = _rms_rows(x1, g_ffn_ref[...]) * (1.0 + modv(4)) + modv(3)
    h2p_ref[...] = _pack_pair(h2[:, :d_model // 2], h2[:, d_model // 2:])

    h2b = h2.astype(BF16)
    gu = _dot(h2b, ws13_ref[...])
    ff = gu.shape[1] // 2
    g = gu[:, :ff]
    act = g * jax.nn.sigmoid(g) * gu[:, ff:]
    shared = _dot(act.astype(BF16), ws2_ref[...])
    xp_ref[...] = x1 + modv(5) * shared

    h2_top = lax.bitcast_convert_type(lax.bitcast_convert_type(h2, jnp.uint32) & jnp.uint32(_HI_MASK), F32)
    h2_hi = h2_top.astype(BF16)
    h2_lo = (h2 - h2_top).astype(BF16)
    logits = _dot_nt(wr_hi_ref[...], h2_hi) + (_dot_nt(wr_hi_ref[...], h2_lo)
                                               + _dot_nt(wr_lo_ref[...], h2_hi))
    s = jax.nn.sigmoid(logits)
    ssel = s + br_ref[...]
    s3 = ssel.reshape(N_GROUPS, per_group, tm)
    m1 = jnp.max(s3, axis=1, keepdims=True)
    eq = s3 == m1
    n_eq = jnp.sum(jnp.where(eq, 1.0, 0.0), axis=1, keepdims=True)
    m2 = jnp.max(jnp.where(eq, NEG_INF, s3), axis=1, keepdims=True)
    grp = m1 + jnp.where(n_eq >= 2.0, m1, m2)

    gi = lax.broadcasted_iota(jnp.int32, grp.shape, 0).astype(F32)
    gcur = grp
    gsel = jnp.zeros_like(grp)
    for _ in range(TOPK_GROUPS):
        gm = jnp.max(gcur, axis=0, keepdims=True)
        first = jnp.min(jnp.where(gcur == gm, gi, float(N_GROUPS)), axis=0, keepdims=True)
        oh = gi == first
        gsel = jnp.where(oh, 1.0, gsel)
        gcur = jnp.where(oh, NEG_INF, gcur)
    cur = jnp.where(gsel > 0.0, s3, NEG_INF).reshape(n_exp, tm)

    ie = lax.broadcasted_iota(jnp.int32, (n_exp, tm), 0).astype(F32)
    sel = jnp.zeros((n_exp, tm), F32)
    e_rows = []
    s_rows = []
    for _ in range(TOP_K):
        m = jnp.max(cur, axis=0, keepdims=True)
        first = jnp.min(jnp.where(cur == m, ie, float(n_exp)), axis=0, keepdims=True)
        oh = ie == first
        cur = jnp.where(oh, NEG_INF, cur)
        sel = jnp.where(oh, 1.0, sel)
        e_rows.append(first)
        s_rows.append(jnp.sum(jnp.where(oh, s, 0.0), axis=0, keepdims=True))
    s_tot = s_rows[0]
    for r in s_rows[1:]:
        s_tot = s_tot + r

    tr = lax.broadcasted_iota(jnp.int32, (tm, tm), 0)
    tc = lax.broadcasted_iota(jnp.int32, (tm, tm), 1)
    upper = jnp.where(tr < tc, 1.0, 0.0).astype(BF16)
    rank = _dot(sel.astype(BF16), upper) + carry_ref[...]
    for k in range(TOP_K):
        eidx_ref[k:k + 1, :] = e_rows[k].astype(jnp.int32)
        rank_ref[k:k + 1, :] = jnp.sum(jnp.where(ie == e_rows[k], rank, 0.0), axis=0,
                                       keepdims=True).astype(jnp.int32)
        wts_ref[k:k + 1, :] = s_rows[k] / s_tot * ROUTED_SCALE
    carry_ref[...] = carry_ref[...] + jnp.sum(sel, axis=1, keepdims=True)
    cnt_ref[...] = carry_ref[...].astype(jnp.int32)


def _post(om, od, x2d, mod, w_out, g_ffn, ws13, ws2, wr_t, br, n_lat, tm):
    t, d = x2d.shape
    n_exp = wr_t.shape[0]
    wr_top = lax.bitcast_convert_type(lax.bitcast_convert_type(wr_t, jnp.uint32) & jnp.uint32(_HI_MASK), F32)
    wr_hi = wr_top.astype(BF16)
    wr_lo = (wr_t - wr_top).astype(BF16)

    def full(a):
        return pl.BlockSpec(a.shape, lambda i: (0,) * a.ndim)

    row = lambda w: pl.BlockSpec((tm, w), lambda i: (i, 0))
    col = pl.BlockSpec((TOP_K, tm), lambda i: (0, i))
    return pl.pallas_call(
        functools.partial(_post_kernel, d, n_lat // tm),
        out_shape=(jax.ShapeDtypeStruct((t, d), F32), jax.ShapeDtypeStruct((t, d // 2), jnp.uint32),
                   jax.ShapeDtypeStruct((TOP_K, t), jnp.int32), jax.ShapeDtypeStruct((TOP_K, t), jnp.int32),
                   jax.ShapeDtypeStruct((TOP_K, t), F32), jax.ShapeDtypeStruct((n_exp, 1), jnp.int32)),
        grid=(t // tm,),
        in_specs=[row(om.shape[1]), row(od.shape[1]), row(d), full(mod), full(w_out), full(g_ffn),
                  full(ws13), full(ws2), full(wr_hi), full(wr_lo), full(br)],
        out_specs=(row(d), row(d // 2), col, col, col, pl.BlockSpec((n_exp, 1), lambda i: (0, 0))),
        scratch_shapes=[pltpu.VMEM((n_exp, 1), F32)],
        compiler_params=_cparams(("arbitrary",)),
        name="post",
    )(om, od, x2d, mod, w_out, g_ffn, ws13, ws2, wr_hi, wr_lo, br)


_HI_MASK = 0xFFFF0000


def _pack_pair(a, b):
    lo = lax.bitcast_convert_type(a.astype(BF16).astype(F32), jnp.uint32) >> 16
    hi = lax.bitcast_convert_type(b.astype(BF16).astype(F32), jnp.uint32) & jnp.uint32(_HI_MASK)
    return lo | hi


def _unpack_pair(w):
    return (lax.bitcast_convert_type(w << 16, F32),
            lax.bitcast_convert_type(w & jnp.uint32(_HI_MASK), F32))


def _row_copy(src, src_row, dst, dst_row, sem):
    return pltpu.make_async_copy(src.at[pl.ds(src_row, 1), :], dst.at[pl.ds(dst_row, 1), :], sem)


def _slots_kernel(start_ref, eidx_ref, rank_ref, dest_ref):
    e = eidx_ref[...]

    def body(x, acc):
        return jnp.where(e == x, start_ref[x], acc)

    dest_ref[...] = lax.fori_loop(0, start_ref.shape[0], body, jnp.zeros_like(e), unroll=8) + rank_ref[...]


def _slots(start, eidx, rank):
    k, t = eidx.shape
    tt = min(t, 2048)
    spec = pl.BlockSpec((k, tt), lambda i, *_: (0, i))
    return pl.pallas_call(
        _slots_kernel,
        out_shape=jax.ShapeDtypeStruct((k, t), jnp.int32),
        grid_spec=pltpu.PrefetchScalarGridSpec(num_scalar_prefetch=1, grid=(t // tt,), in_specs=[spec, spec],
                                               out_specs=spec),
        compiler_params=_cparams(("arbitrary",)),
        name="slots",
    )(start, eidx, rank)


def _dest_copy(t, td, dest_hbm, d_s, sem):
    return pltpu.make_async_copy(dest_hbm.at[:, pl.ds(t * td, td)], d_s.at[t % 2], sem.at[t % 2])


def _dispatch_kernel(h2p_hbm, dest_hbm, xs_hbm, hbuf, d_s, fsem, rsem, dsem):
    i = pl.program_id(0)
    n = pl.num_programs(0)
    td = hbuf.shape[1]

    def fetch(t):
        slot = t % 3
        return pltpu.make_async_copy(h2p_hbm.at[pl.ds(t * td, td), :], hbuf.at[slot], fsem.at[slot])

    def wait_rows(t):
        for _ in range(td * TOP_K):
            _row_copy(hbuf.at[0], 0, xs_hbm, 0, dsem.at[t % 2]).wait()

    @pl.when(i == 0)
    def _():
        fetch(0).start()
        _dest_copy(0, td, dest_hbm, d_s, rsem).start()

        @pl.when(n > 1)
        def _():
            fetch(1).start()

    @pl.when(i + 1 < n)
    def _():
        _dest_copy(i + 1, td, dest_hbm, d_s, rsem).start()

    fetch(i).wait()
    _dest_copy(i, td, dest_hbm, d_s, rsem).wait()

    src = hbuf.at[i % 3]
    dst = d_s.at[i % 2]
    sem = dsem.at[i % 2]
    for j in range(td):
        for k in range(TOP_K):
            _row_copy(src, j, xs_hbm, dst[k, j], sem).start()

    @pl.when(i >= 1)
    def _():
        wait_rows(i - 1)

    @pl.when(i + 2 < n)
    def _():
        fetch(i + 2).start()

    @pl.when(i == n - 1)
    def _():
        wait_rows(i)


def _dispatch(h2p, dest, td):
    t, dw = h2p.shape
    any_spec = pl.BlockSpec(memory_space=pl.ANY)
    return pl.pallas_call(
        _dispatch_kernel,
        out_shape=jax.ShapeDtypeStruct((t * TOP_K, dw), jnp.uint32),
        grid=(t // td,),
        in_specs=[any_spec, any_spec],
        out_specs=any_spec,
        scratch_shapes=[pltpu.VMEM((3, td, dw), jnp.uint32), pltpu.SMEM((2, TOP_K, td), jnp.int32),
                        pltpu.SemaphoreType.DMA((3,)), pltpu.SemaphoreType.DMA((2,)),
                        pltpu.SemaphoreType.DMA((2,))],
        compiler_params=_cparams(("arbitrary",)),
        name="dispatch",
    )(h2p, dest)


def _experts_kernel(first_ref, e_ref, lo_ref, hi_ref, slot_ref, next_ref, xs_ref, w1_hbm, w3_hbm, w2_hbm, ys_ref,
                    w1_f, w3_f, w2_f, w13_b, w2_b, wsem):
    b = pl.program_id(0)
    bm, dw = xs_ref.shape
    ff = w1_hbm.shape[2]
    base = b * bm

    def fetch(e, slot):
        return (pltpu.make_async_copy(w1_hbm.at[e], w1_f.at[slot], wsem.at[slot, 0]),
                pltpu.make_async_copy(w3_hbm.at[e], w3_f.at[slot], wsem.at[slot, 1]),
                pltpu.make_async_copy(w2_hbm.at[e], w2_f.at[slot], wsem.at[slot, 2]))

    def swiglu():
        x_lo, x_hi = _unpack_pair(xs_ref[...])
        gu = _dot(x_lo.astype(BF16), w13_b[:dw, :]) + _dot(x_hi.astype(BF16), w13_b[dw:, :])
        g = gu[:, :ff]
        act = g * jax.nn.sigmoid(g) * gu[:, ff:]
        y = _dot(act.astype(BF16), w2_b[...])
        return _pack_pair(y[:, :dw], y[:, dw:])

    def item(j, carry):
        lo = lo_ref[j]
        hi = hi_ref[j]

        @pl.when(j == 0)
        def _():
            for cp in fetch(e_ref[0], 0):
                cp.start()

        @pl.when(jnp.logical_or(j == 0, e_ref[j] != e_ref[jnp.maximum(j - 1, 0)]))
        def _():
            slot = slot_ref[j]
            for cp in fetch(e_ref[j], slot):
                cp.wait()

            @pl.when(next_ref[j] >= 0)
            def _():
                for cp in fetch(next_ref[j], 1 - slot):
                    cp.start()

            w13_b[:, :ff] = w1_f[slot].astype(BF16)
            w13_b[:, ff:] = w3_f[slot].astype(BF16)
            w2_b[...] = w2_f[slot].astype(BF16)

        @pl.when(jnp.logical_and(hi > lo, lo == base))
        def _():
            ys_ref[...] = swiglu()

        @pl.when(jnp.logical_and(hi > lo, lo != base))
        def _():
            rows = base + lax.broadcasted_iota(jnp.int32, (bm, 1), 0)
            ys_ref[...] = jnp.where(rows >= lo, swiglu(), ys_ref[...])

        return carry

    lax.fori_loop(first_ref[b], first_ref[b + 1], item, 0)


def _experts(items, xs, w1, w3, w2, bm):
    n_slots, dw = xs.shape
    _, d, ff = w1.shape
    any_spec = pl.BlockSpec(memory_space=pl.ANY)
    rows = pl.BlockSpec((bm, dw), lambda b, *_: (b, 0))
    return pl.pallas_call(
        _experts_kernel,
        out_shape=jax.ShapeDtypeStruct((n_slots, dw), jnp.uint32),
        grid_spec=pltpu.PrefetchScalarGridSpec(
            num_scalar_prefetch=len(items),
            grid=(n_slots // bm,),
            in_specs=[rows, any_spec, any_spec, any_spec],
            out_specs=rows,
            scratch_shapes=[pltpu.VMEM((2, d, ff), F32), pltpu.VMEM((2, d, ff), F32), pltpu.VMEM((2, ff, d), F32),
                            pltpu.VMEM((d, 2 * ff), BF16), pltpu.VMEM((ff, d), BF16),
                            pltpu.SemaphoreType.DMA((2, 3))]),
        compiler_params=_cparams(("arbitrary",)),
        name="experts",
    )(*items, xs, w1, w3, w2)


def _combine_kernel(d_model, tiles_per_batch, xp_ref, wts_ref, mod_ref, g_final_ref, dest_hbm, ys_hbm, o_ref,
                    gbuf, d_s, rsem, gsem):
    s = pl.program_id(0)
    n = pl.num_programs(0) - 1
    tc = xp_ref.shape[0]
    dw = d_model // 2

    @pl.when(s == 0)
    def _():
        _dest_copy(0, tc, dest_hbm, d_s, rsem).start()

    @pl.when(s + 1 < n)
    def _():
        _dest_copy(s + 1, tc, dest_hbm, d_s, rsem).start()

    @pl.when(s < n)
    def _():
        _dest_copy(s, tc, dest_hbm, d_s, rsem).wait()
        src = d_s.at[s % 2]
        dst = gbuf.at[s % 2]
        sem = gsem.at[s % 2]
        for j in range(tc):
            for k in range(TOP_K):
                _row_copy(ys_hbm, src[k, j], dst.at[k], j, sem).start()

    @pl.when(s >= 1)
    def _():
        buf = (s - 1) % 2
        for _ in range(tc * TOP_K):
            _row_copy(ys_hbm, 0, gbuf.at[0, 0], 0, gsem.at[buf]).wait()
        w = wts_ref[...]
        r_lo = r_hi = None
        for k in range(TOP_K):
            y_lo, y_hi = _unpack_pair(gbuf[buf, k])
            wk = w[:, k:k + 1]
            r_lo = wk * y_lo if r_lo is None else r_lo + wk * y_lo
            r_hi = wk * y_hi if r_hi is None else r_hi + wk * y_hi
        b = (s - 1) // tiles_per_batch
        gt2 = mod_ref[pl.ds(b, 1), 5 * d_model:6 * d_model]
        v_lo = xp_ref[:, :dw] + gt2[:, :dw] * r_lo
        v_hi = xp_ref[:, dw:] + gt2[:, dw:] * r_hi
        ms = (jnp.sum(v_lo * v_lo, axis=-1, keepdims=True)
              + jnp.sum(v_hi * v_hi, axis=-1, keepdims=True)) / d_model
        inv = lax.rsqrt(ms + NORM_EPS)
        o_ref[:, :dw] = v_lo * inv * g_final_ref[:, :dw]
        o_ref[:, dw:] = v_hi * inv * g_final_ref[:, dw:]


def _combine(xp, wts_t, mod, g_final, dest, ys, n_lat, tc):
    t, d = xp.shape
    any_spec = pl.BlockSpec(memory_space=pl.ANY)
    prev = lambda s: (jnp.maximum(s - 1, 0), 0)
    return pl.pallas_call(
        functools.partial(_combine_kernel, d, n_lat // tc),
        out_shape=jax.ShapeDtypeStruct((t, d), F32),
        grid=(t // tc + 1,),
        in_specs=[pl.BlockSpec((tc, d), prev), pl.BlockSpec((tc, TOP_K), prev),
                  pl.BlockSpec(mod.shape, lambda s: (0, 0)), pl.BlockSpec(g_final.shape, lambda s: (0, 0)),
                  any_spec, any_spec],
        out_specs=pl.BlockSpec((tc, d), prev),
        scratch_shapes=[pltpu.VMEM((2, TOP_K, tc, d // 2), jnp.uint32), pltpu.SMEM((2, TOP_K, tc), jnp.int32),
                        pltpu.SemaphoreType.DMA((2,)), pltpu.SemaphoreType.DMA((2,))],
        compiler_params=_cparams(("arbitrary",)),
        name="combine",
    )(xp, wts_t, mod, g_final, dest, ys)


def _swap_halves(w, width):
    r, c = w.shape
    return w.reshape(r, c // width, 2, width // 2)[:, :, ::-1, :].reshape(r, c)


def _pad_cols(w, left, total):
    return jnp.pad(w, ((0, 0), (left, total - left - w.shape[1])))


def _layout_weights(w_in, w_uq, w_ukv):
    s0 = MLA_Q_RANK
    s1 = s0 + MLA_KV_RANK
    s2 = s1 + MLA_ROPE
    s3 = s2 + DIFF_HEADS * 2 * DIFF_QK
    s4 = s3 + DIFF_HEADS * 2 * DIFF_QK
    log2e = 1.0 / math.log(2.0)
    w_kr, w_dq, w_dk = w_in[:, s1:s2], w_in[:, s2:s3] * (log2e / math.sqrt(DIFF_QK)), w_in[:, s3:s4]
    w_uq = w_uq * (log2e / math.sqrt(MLA_NOPE + MLA_ROPE))
    w_in_ext = jnp.concatenate([
        w_in[:, :s1], w_dq, _swap_halves(w_dq, DIFF_QK), w_dk, _swap_halves(w_dk, DIFF_QK), w_in[:, s4:],
        _pad_cols(w_kr, MLA_NOPE, LANES), _pad_cols(_swap_halves(w_kr, MLA_ROPE), MLA_NOPE, LANES)],
        axis=1).astype(BF16)

    qk = MLA_NOPE + MLA_ROPE
    qa, qb, kk, vv = [], [], [], []
    for h in range(MLA_HEADS):
        wq = w_uq[:, h * qk:(h + 1) * qk]
        qa.append(_pad_cols(wq, 0, LANES))
        qb.append(_pad_cols(_swap_halves(wq[:, MLA_NOPE:], MLA_ROPE), MLA_NOPE, LANES))
        wkv = w_ukv[:, h * (MLA_NOPE + MLA_V):(h + 1) * (MLA_NOPE + MLA_V)]
        kk.append(_pad_cols(wkv[:, :MLA_NOPE], 0, LANES))
        vv.append(wkv[:, MLA_NOPE:])
    w_uq_ext = jnp.concatenate(qa + qb, axis=1).astype(BF16)
    w_ukv_ext = jnp.concatenate(kk + vv, axis=1).astype(BF16)
    return w_in_ext, w_uq_ext, w_ukv_ext


def _rotary_tables(n_ctx, n_lat):
    def angles(rot_dim):
        n_freq = rot_dim // 4
        inv = ROPE_BASE ** (-(np.arange(n_freq, dtype=np.float64) / n_freq))
        rows = n_lat // GRID_W
        row = np.repeat(np.arange(rows, dtype=np.float64), GRID_W)
        col = np.tile(np.arange(GRID_W, dtype=np.float64), rows)
        theta = np.concatenate([row[:, None] * inv, col[:, None] * inv], axis=-1)
        theta = np.concatenate([np.zeros((n_ctx, 2 * n_freq)), theta], axis=0)
        return np.cos(theta), np.sin(theta)

    n = n_ctx + n_lat
    cm, sm = angles(MLA_ROPE)
    cd, sd = angles(DIFF_QK)
    pad_m = LANES - MLA_NOPE - MLA_ROPE
    cq = np.concatenate([np.ones((n, MLA_NOPE)), cm, cm, np.zeros((n, pad_m))], axis=1)
    sq = np.concatenate([np.zeros((n, MLA_NOPE)), -sm, sm, np.zeros((n, pad_m))], axis=1)
    cdd = np.concatenate([cd, cd, cd, cd], axis=1)
    sdd = np.concatenate([-sd, sd, -sd, sd], axis=1)
    return jnp.asarray(np.concatenate([cq, sq, cdd, sdd], axis=1), F32)


def kernel(x, c, ctx, c_ctx, w_mod, b_mod, g_attn, g_ffn, w_in, g_q_lat, w_uq, g_kv_lat, w_ukv, lam_q1, lam_k1,
           lam_q2, lam_k2, g_subln, w_out, w_router, router_bias, w1, w3, w2, ws1, ws3, ws2, g_final):
    n_batch, n_lat, d = x.shape
    n_ctx = ctx.shape[1]
    t = n_batch * n_lat
    tm = 256
    tq = 512
    bm = 256
    td = 128
    assert w_mod.shape[0] == 1 and n_ctx % tm == 0 and n_lat % tq == 0 and n_lat % tm == 0 and t % td == 0
    assert (t * TOP_K) % bm == 0

    cc = jnp.concatenate([c, c_ctx[None, :], jnp.zeros((8 - n_batch - 1, d), F32)], axis=0)
    mod = _modulation(cc, w_mod[0], b_mod)

    w_in_ext, w_uq_ext, w_ukv_ext = _layout_weights(w_in[0], w_uq[0], w_ukv[0])
    tables = _rotary_tables(n_ctx, n_lat)
    qm, km, vmt, qd, k12, vdt = _projections(ctx, x, mod, g_attn, w_in_ext, g_q_lat, w_uq_ext, g_kv_lat,
                                             w_ukv_ext, tables, tm)
    om = _attn_mla(qm, km, vmt, tq)
    od = _attn_diff((lam_q1, lam_k1, lam_q2, lam_k2), g_subln.reshape(-1, 1), qd, k12, vdt, tq)

    ws13 = jnp.concatenate([ws1[0], ws3[0]], axis=1).astype(BF16)
    xp, h2p, eidx, rank, wts, cnt = _post(
        om.reshape(t, -1), od.reshape(t, -1), x.reshape(t, d), mod, w_out[0].astype(BF16), g_ffn, ws13,
        ws2[0].astype(BF16), w_router[0].T, router_bias[0][:, None], n_lat, tm)

    counts = cnt[:, 0]
    start = (jnp.cumsum(counts) - counts).astype(jnp.int32)
    n_slots = t * TOP_K
    n_blocks = n_slots // bm
    blk_b = jnp.arange(n_blocks, dtype=jnp.int32) * bm
    exp_b = jnp.concatenate([start[1:], jnp.full((1,), n_slots, jnp.int32)])
    pos_blk = jnp.arange(n_blocks, dtype=jnp.int32) + jnp.sum(exp_b[None, :] < blk_b[:, None], axis=1)
    pos_exp = jnp.arange(N_EXPERTS, dtype=jnp.int32) + jnp.sum(blk_b[None, :] <= exp_b[:, None], axis=1)
    slot = jnp.arange(n_blocks + N_EXPERTS, dtype=jnp.int32)[:, None]
    bounds = (jnp.sum(jnp.where(pos_blk[None, :] == slot, blk_b[None, :], 0), axis=1)
              + jnp.sum(jnp.where(pos_exp[None, :] == slot, exp_b[None, :], 0), axis=1))
    item_lo, item_hi = bounds[:-1], bounds[1:]
    item_blk = jnp.minimum(item_lo // bm, n_blocks - 1)
    item_e = jnp.clip(jnp.sum(start[None, :] <= item_lo[:, None], axis=1) - 1, 0, N_EXPERTS - 1).astype(jnp.int32)
    changed = jnp.concatenate([jnp.zeros((1,), jnp.int32), (item_e[1:] != item_e[:-1]).astype(jnp.int32)])
    item_slot = jnp.cumsum(changed).astype(jnp.int32) % 2
    later = jnp.where(item_e[None, :] > item_e[:, None], item_e[None, :], N_EXPERTS)
    item_next = jnp.min(later, axis=1)
    item_next = jnp.where(item_next == N_EXPERTS, -1, item_next).astype(jnp.int32)

    dest = _slots(start, eidx, rank)
    xs = _dispatch(h2p, dest, td)
    blk_ids = jnp.arange(n_blocks + 1, dtype=jnp.int32)
    item_first = jnp.sum(item_blk[None, :] < blk_ids[:, None], axis=1).astype(jnp.int32)
    ys = _experts((item_first, item_e, item_lo, item_hi, item_slot, item_next), xs, w1[0], w3[0], w2[0], bm)
    out = _combine(xp, wts.T, mod, g_final[None, :], dest, ys, n_lat, td)
    return out.reshape(n_batch, n_lat, d)
```

```python
import functools
import math

import jax
import jax.numpy as jnp
import numpy as np
from jax import lax
from jax.experimental import pallas as pl
from jax.experimental.pallas import tpu as pltpu

GRID_W = 64
ROPE_BASE = 10000.0
NORM_EPS = 1e-6
MLA_HEADS = 8
MLA_NOPE = 64
MLA_ROPE = 32
MLA_V = 64
MLA_Q_RANK = 256
MLA_KV_RANK = 128
DIFF_HEADS = 4
DIFF_QK = 64
DIFF_V = 2 * DIFF_QK
N_EXPERTS = 256
TOP_K = 8
N_GROUPS = 8
TOPK_GROUPS = 4
ROUTED_SCALE = 2.5
LAM_INIT = 0.8 - 0.6 * math.exp(-0.3 * 0)

LANES = 128
VMEM_LIMIT = 48 * 1024 * 1024

F32 = jnp.float32
BF16 = jnp.bfloat16
NEG_INF = float("-inf")


def _cparams(sem):
    return pltpu.CompilerParams(dimension_semantics=sem, vmem_limit_bytes=VMEM_LIMIT)


def _rms_rows(x, g):
    return x * lax.rsqrt(jnp.mean(x * x, axis=-1, keepdims=True) + NORM_EPS) * g


def _dot(a, b):
    return jnp.dot(a, b, preferred_element_type=F32)


def _dot_nt(a, b):
    return lax.dot_general(a, b, (((1,), (1,)), ((), ())), preferred_element_type=F32)


def _mod_kernel(c_ref, w_ref, b_ref, o_ref):
    a = c_ref[...]
    a = a * jax.nn.sigmoid(a)
    o_ref[...] = jnp.dot(a, w_ref[...], preferred_element_type=F32,
                         precision=lax.Precision.HIGHEST) + b_ref[...]


def _modulation(cc, w_mod, b_mod):
    rows, d = cc.shape
    cols = w_mod.shape[1]
    tn = 1536
    return pl.pallas_call(
        _mod_kernel,
        out_shape=jax.ShapeDtypeStruct((rows, cols), F32),
        grid=(cols // tn,),
        in_specs=[pl.BlockSpec((rows, d), lambda j: (0, 0)),
                  pl.BlockSpec((d, tn), lambda j: (0, j)),
                  pl.BlockSpec((1, tn), lambda j: (0, j))],
        out_specs=pl.BlockSpec((rows, tn), lambda j: (0, j)),
        compiler_params=_cparams(("arbitrary",)),
        name="mod",
    )(cc, w_mod, b_mod)


_C_Q = 0
_C_KV = _C_Q + MLA_Q_RANK
_C_DQ = _C_KV + MLA_KV_RANK
_C_DQS = _C_DQ + DIFF_HEADS * 2 * DIFF_QK
_C_DK = _C_DQS + DIFF_HEADS * 2 * DIFF_QK
_C_DKS = _C_DK + DIFF_HEADS * 2 * DIFF_QK
_C_DV = _C_DKS + DIFF_HEADS * 2 * DIFF_QK
_C_KRA = _C_DV + DIFF_HEADS * DIFF_V
_C_KRB = _C_KRA + LANES
_IN_EXT = _C_KRB + LANES
_N_TAB = 4


def _proj_kernel(n_batch, d_model, ctx_ref, x_ref, mod_ref, g_attn_ref, w_in_ref, g_q_ref, w_uq_ref,
                 g_kv_ref, w_ukv_ref, tab_ref,
                 qm_ref, km_ref, vmt_ref, qd_ref, k12_ref, vdt_ref, *, n_ctx_tiles):
    b = pl.program_id(0)
    i = pl.program_id(1)
    is_ctx = i < n_ctx_tiles
    xin = jnp.where(is_ctx, ctx_ref[0], x_ref[0])
    row = jnp.where(is_ctx, n_batch, b)
    sh1 = mod_ref[pl.ds(row, 1), 0:d_model]
    sc1 = mod_ref[pl.ds(row, 1), d_model:2 * d_model]
    h = _rms_rows(xin, g_attn_ref[...]) * (1.0 + sc1) + sh1
    p = _dot(h.astype(BF16), w_in_ref[...])

    cq = _rms_rows(p[:, _C_Q:_C_Q + MLA_Q_RANK], g_q_ref[...])
    qa = _dot(cq.astype(BF16), w_uq_ref[...])
    ckv = _rms_rows(p[:, _C_KV:_C_KV + MLA_KV_RANK], g_kv_ref[...])
    kv = _dot(ckv.astype(BF16), w_ukv_ref[...])

    def tab(j):
        return tab_ref[:, j * LANES:(j + 1) * LANES]

    kr = p[:, _C_KRA:_C_KRA + LANES] * tab(0) + p[:, _C_KRB:_C_KRB + LANES] * tab(1)
    first_half = lax.broadcasted_iota(jnp.int32, (1, LANES), 1) < DIFF_QK
    hw = MLA_HEADS * LANES
    for hd in range(MLA_HEADS):
        lo = hd * LANES
        qm_ref[0, hd] = (qa[:, lo:lo + LANES] * tab(0) + qa[:, hw + lo:hw + lo + LANES] * tab(1)).astype(BF16)
        km_ref[0, hd] = (kv[:, lo:lo + LANES] + kr).astype(BF16)
    for pr in range(MLA_HEADS // 2):
        lo = hw + pr * LANES
        vmt_ref[0, pr] = kv[:, lo:lo + LANES].T.astype(BF16)
    for hd in range(DIFF_HEADS):
        lo = hd * LANES
        dq = p[:, _C_DQ + lo:_C_DQ + lo + LANES]
        dqs = p[:, _C_DQS + lo:_C_DQS + lo + LANES]
        qr = dq * tab(2) + dqs * tab(3)
        qd_ref[0, hd, 0] = jnp.where(first_half, qr, 0.0).astype(BF16)
        qd_ref[0, hd, 1] = jnp.where(first_half, 0.0, qr).astype(BF16)
        dk = p[:, _C_DK + lo:_C_DK + lo + LANES]
        dks = p[:, _C_DKS + lo:_C_DKS + lo + LANES]
        k12_ref[0, hd] = (dk * tab(2) + dks * tab(3)).astype(BF16)
        vdt_ref[0, hd] = p[:, _C_DV + lo:_C_DV + lo + LANES].T.astype(BF16)


def _projections(ctx, x, mod, g_attn, w_in_ext, g_q, w_uq_ext, g_kv, w_ukv_ext, tables, tm):
    n_batch, n_ctx, d = ctx.shape
    n_lat = x.shape[1]
    n_all = n_ctx + n_lat
    nct = n_ctx // tm
    nt = n_all // tm

    def full(a):
        return pl.BlockSpec(a.shape, lambda b, i: (0,) * a.ndim)

    def k_spec(heads):
        return pl.BlockSpec((1, heads, tm, LANES), lambda b, i: (b, 0, i, 0))

    def vt_spec(heads):
        return pl.BlockSpec((1, heads, LANES, tm), lambda b, i: (b, 0, 0, i))

    def lat(i):
        return jnp.maximum(i - nct, 0)

    def sds(*shape):
        return jax.ShapeDtypeStruct((n_batch,) + shape, BF16)

    return pl.pallas_call(
        functools.partial(_proj_kernel, n_batch, d, n_ctx_tiles=nct),
        out_shape=(sds(MLA_HEADS, n_lat, LANES), sds(MLA_HEADS, n_all, LANES), sds(MLA_HEADS // 2, LANES, n_all),
                   sds(DIFF_HEADS, 2, n_lat, LANES), sds(DIFF_HEADS, n_all, LANES),
                   sds(DIFF_HEADS, LANES, n_all)),
        grid=(n_batch, nt),
        in_specs=[pl.BlockSpec((1, tm, d), lambda b, i: (b, jnp.minimum(i, nct - 1), 0)),
                  pl.BlockSpec((1, tm, d), lambda b, i: (b, jnp.maximum(i - nct, 0), 0)),
                  full(mod), full(g_attn), full(w_in_ext), full(g_q), full(w_uq_ext), full(g_kv),
                  full(w_ukv_ext),
                  pl.BlockSpec((tm, _N_TAB * LANES), lambda b, i: (i, 0))],
        out_specs=(pl.BlockSpec((1, MLA_HEADS, tm, LANES), lambda b, i: (b, 0, lat(i), 0)),
                   k_spec(MLA_HEADS), vt_spec(MLA_HEADS // 2),
                   pl.BlockSpec((1, DIFF_HEADS, 2, tm, LANES), lambda b, i: (b, 0, 0, lat(i), 0)),
                   k_spec(DIFF_HEADS), vt_spec(DIFF_HEADS)),
        compiler_params=_cparams(("arbitrary", "arbitrary")),
        name="proj",
    )(ctx, x, mod, g_attn, w_in_ext, g_q, w_uq_ext, g_kv, w_ukv_ext, tables)


def _attend_t(q, k, vt):
    st = _dot_nt(k, q)
    et = jnp.exp2(st - jnp.max(st, axis=0, keepdims=True))
    l = jnp.sum(et, axis=0, keepdims=True)
    return _dot(vt, et.astype(BF16)) * (1.0 / l)


def _attn_mla_kernel(q_ref, k_ref, vt_ref, o_ref):
    sts = [_dot_nt(k_ref[0, j], q_ref[0, j]) for j in range(2)]
    ets = [jnp.exp2(st - jnp.max(st, axis=0, keepdims=True)) for st in sts]
    ls = [jnp.sum(et, axis=0, keepdims=True) for et in ets]
    outs = [_dot(vt_ref[0, 0, j * MLA_V:(j + 1) * MLA_V, :], ets[j].astype(BF16)) * (1.0 / ls[j]) for j in range(2)]
    o_ref[0] = jnp.concatenate(outs, axis=0).T.astype(BF16)


def _attn_mla(qm, km, vmt, tq):
    n_batch, heads, n_lat, _ = qm.shape
    n_all = km.shape[2]
    return pl.pallas_call(
        _attn_mla_kernel,
        out_shape=jax.ShapeDtypeStruct((n_batch, n_lat, heads // 2 * LANES), BF16),
        grid=(n_batch, heads // 2, n_lat // tq),
        in_specs=[pl.BlockSpec((1, 2, tq, LANES), lambda b, h, i: (b, h, i, 0)),
                  pl.BlockSpec((1, 2, n_all, LANES), lambda b, h, i: (b, h, 0, 0)),
                  pl.BlockSpec((1, 1, LANES, n_all), lambda b, h, i: (b, h, 0, 0))],
        out_specs=pl.BlockSpec((1, tq, LANES), lambda b, h, i: (b, i, h)),
        compiler_params=_cparams(("arbitrary", "arbitrary", "arbitrary")),
        name="attn_m",
    )(qm, km, vmt)


def _attn_diff_kernel(lq1_ref, lk1_ref, lq2_ref, lk2_ref, g_sub_ref, q_ref, k_ref, vt_ref, o_ref):
    lam = (jnp.exp(jnp.sum(lq1_ref[...] * lk1_ref[...], axis=-1, keepdims=True))
           - jnp.exp(jnp.sum(lq2_ref[...] * lk2_ref[...], axis=-1, keepdims=True)) + LAM_INIT)
    tq = q_ref.shape[3]
    o = _attend_t(q_ref[0, 0].reshape(2 * tq, LANES), k_ref[0, 0], vt_ref[0, 0])
    ot = o[:, :tq] - lam * o[:, tq:]
    ot = ot * lax.rsqrt(jnp.mean(ot * ot, axis=0, keepdims=True) + NORM_EPS) * g_sub_ref[...]
    o_ref[0] = (ot * (1.0 - LAM_INIT)).T.astype(BF16)


def _attn_diff(lams, g_sub_col, qd, k12, vdt, tq):
    n_batch, heads, _, n_lat, _ = qd.shape
    n_all = k12.shape[2]

    def small(a):
        return pl.BlockSpec(a.shape, lambda b, h, i: (0,) * a.ndim)

    return pl.pallas_call(
        _attn_diff_kernel,
        out_shape=jax.ShapeDtypeStruct((n_batch, n_lat, heads * LANES), BF16),
        grid=(n_batch, heads, n_lat // tq),
        in_specs=[small(lams[0]), small(lams[1]), small(lams[2]), small(lams[3]), small(g_sub_col),
                  pl.BlockSpec((1, 1, 2, tq, LANES), lambda b, h, i: (b, h, 0, i, 0)),
                  pl.BlockSpec((1, 1, n_all, LANES), lambda b, h, i: (b, h, 0, 0)),
                  pl.BlockSpec((1, 1, LANES, n_all), lambda b, h, i: (b, h, 0, 0))],
        out_specs=pl.BlockSpec((1, tq, LANES), lambda b, h, i: (b, i, h)),
        compiler_params=_cparams(("arbitrary", "arbitrary", "arbitrary")),
        name="attn_d",
    )(*lams, g_sub_col, qd, k12, vdt)


def _post_kernel(d_model, tiles_per_batch, om_ref, od_ref, x_ref, mod_ref, w_out_ref, g_ffn_ref, ws13_ref,
                 ws2_ref, wr_hi_ref, wr_lo_ref, br_ref,
                 xp_ref, h2p_ref, eidx_ref, rank_ref, wts_ref, cnt_ref, carry_ref):
    i = pl.program_id(0)
    tm = x_ref.shape[0]
    n_exp = wr_hi_ref.shape[0]
    per_group = n_exp // N_GROUPS

    @pl.when(i == 0)
    def _():
        carry_ref[...] = jnp.zeros_like(carry_ref)

    b = i // tiles_per_batch

    def modv(j):
        return mod_ref[pl.ds(b, 1), j * d_model:(j + 1) * d_model]

    half = om_ref.shape[1]
    y = _dot(om_ref[...], w_out_ref[0:half, :]) + _dot(od_ref[...], w_out_ref[half:, :])
    x1 = x_ref[...] + modv(2) * y
    h2 = _rms_rows(x1, g_ffn_ref[...]) * (1.0 + modv(4)) + modv(3)
    h2p_ref[...] = _pack_pair(h2[:, :d_model // 2], h2[:, d_model // 2:])

    h2b = h2.astype(BF16)
    gu = _dot(h2b, ws13_ref[...])
    ff = gu.shape[1] // 2
    g = gu[:, :ff]
    act = g * jax.nn.sigmoid(g) * gu[:, ff:]
    shared = _dot(act.astype(BF16), ws2_ref[...])
    xp_ref[...] = x1 + modv(5) * shared

    h2_top = lax.bitcast_convert_type(lax.bitcast_convert_type(h2, jnp.uint32) & jnp.uint32(_HI_MASK), F32)
    h2_hi = h2_top.astype(BF16)
    h2_lo = (h2 - h2_top).astype(BF16)
    logits = _dot_nt(wr_hi_ref[...], h2_hi) + (_dot_nt(wr_hi_ref[...], h2_lo)
                                               + _dot_nt(wr_lo_ref[...], h2_hi))
    s = jax.nn.sigmoid(logits)
    ssel = s + br_ref[...]
    s3 = ssel.reshape(N_GROUPS, per_group, tm)
    m1 = jnp.max(s3, axis=1, keepdims=True)
    eq = s3 == m1
    n_eq = jnp.sum(jnp.where(eq, 1.0, 0.0), axis=1, keepdims=True)
    m2 = jnp.max(jnp.where(eq, NEG_INF, s3), axis=1, keepdims=True)
    grp = m1 + jnp.where(n_eq >= 2.0, m1, m2)

    gi = lax.broadcasted_iota(jnp.int32, grp.shape, 0).astype(F32)
    gcur = grp
    gsel = jnp.zeros_like(grp)
    for _ in range(TOPK_GROUPS):
        gm = jnp.max(gcur, axis=0, keepdims=True)
        first = jnp.min(jnp.where(gcur == gm, gi, float(N_GROUPS)), axis=0, keepdims=True)
        oh = gi == first
        gsel = jnp.where(oh, 1.0, gsel)
        gcur = jnp.where(oh, NEG_INF, gcur)
    cur = jnp.where(gsel > 0.0, s3, NEG_INF).reshape(n_exp, tm)

    ie = lax.broadcasted_iota(jnp.int32, (n_exp, tm), 0).astype(F32)
    sel = jnp.zeros((n_exp, tm), F32)
    e_rows = []
    s_rows = []
    for _ in range(TOP_K):
        m = jnp.max(cur, axis=0, keepdims=True)
        first = jnp.min(jnp.where(cur == m, ie, float(n_exp)), axis=0, keepdims=True)
        oh = ie == first
        cur = jnp.where(oh, NEG_INF, cur)
        sel = jnp.where(oh, 1.0, sel)
        e_rows.append(first)
        s_rows.append(jnp.sum(jnp.where(oh, s, 0.0), axis=0, keepdims=True))
    s_tot = s_rows[0]
    for r in s_rows[1:]:
        s_tot = s_tot + r

    tr = lax.broadcasted_iota(jnp.int32, (tm, tm), 0)
    tc = lax.broadcasted_iota(jnp.int32, (tm, tm), 1)
    upper = jnp.where(tr < tc, 1.0, 0.0).astype(BF16)
    rank = _dot(sel.astype(BF16), upper) + carry_ref[...]
    for k in range(TOP_K):
        eidx_ref[k:k + 1, :] = e_rows[k].astype(jnp.int32)
        rank_ref[k:k + 1, :] = jnp.sum(jnp.where(ie == e_rows[k], rank, 0.0), axis=0,
                                       keepdims=True).astype(jnp.int32)
        wts_ref[k:k + 1, :] = s_rows[k] / s_tot * ROUTED_SCALE
    carry_ref[...] = carry_ref[...] + jnp.sum(sel, axis=1, keepdims=True)
    cnt_ref[...] = carry_ref[...].astype(jnp.int32)


def _post(om, od, x2d, mod, w_out, g_ffn, ws13, ws2, wr_t, br, n_lat, tm):
    t, d = x2d.shape
    n_exp = wr_t.shape[0]
    wr_top = lax.bitcast_convert_type(lax.bitcast_convert_type(wr_t, jnp.uint32) & jnp.uint32(_HI_MASK), F32)
    wr_hi = wr_top.astype(BF16)
    wr_lo = (wr_t - wr_top).astype(BF16)

    def full(a):
        return pl.BlockSpec(a.shape, lambda i: (0,) * a.ndim)

    row = lambda w: pl.BlockSpec((tm, w), lambda i: (i, 0))
    col = pl.BlockSpec((TOP_K, tm), lambda i: (0, i))
    return pl.pallas_call(
        functools.partial(_post_kernel, d, n_lat // tm),
        out_shape=(jax.ShapeDtypeStruct((t, d), F32), jax.ShapeDtypeStruct((t, d // 2), jnp.uint32),
                   jax.ShapeDtypeStruct((TOP_K, t), jnp.int32), jax.ShapeDtypeStruct((TOP_K, t), jnp.int32),
                   jax.ShapeDtypeStruct((TOP_K, t), F32), jax.ShapeDtypeStruct((n_exp, 1), jnp.int32)),
        grid=(t // tm,),
        in_specs=[row(om.shape[1]), row(od.shape[1]), row(d), full(mod), full(w_out), full(g_ffn),
                  full(ws13), full(ws2), full(wr_hi), full(wr_lo), full(br)],
        out_specs=(row(d), row(d // 2), col, col, col, pl.BlockSpec((n_exp, 1), lambda i: (0, 0))),
        scratch_shapes=[pltpu.VMEM((n_exp, 1), F32)],
        compiler_params=_cparams(("arbitrary",)),
        name="post",
    )(om, od, x2d, mod, w_out, g_ffn, ws13, ws2, wr_hi, wr_lo, br)


_HI_MASK = 0xFFFF0000


def _pack_pair(a, b):
    lo = lax.bitcast_convert_type(a.astype(BF16).astype(F32), jnp.uint32) >> 16
    hi = lax.bitcast_convert_type(b.astype(BF16).astype(F32), jnp.uint32) & jnp.uint32(_HI_MASK)
    return lo | hi


def _unpack_pair(w):
    return (lax.bitcast_convert_type(w << 16, F32),
            lax.bitcast_convert_type(w & jnp.uint32(_HI_MASK), F32))


def _row_copy(src, src_row, dst, dst_row, sem):
    return pltpu.make_async_copy(src.at[pl.ds(src_row, 1), :], dst.at[pl.ds(dst_row, 1), :], sem)


def _slots_kernel(start_ref, eidx_ref, rank_ref, dest_ref):
    e = eidx_ref[...]

    def body(x, acc):
        return jnp.where(e == x, start_ref[x], acc)

    dest_ref[...] = lax.fori_loop(0, start_ref.shape[0], body, jnp.zeros_like(e), unroll=8) + rank_ref[...]


def _slots(start, eidx, rank):
    k, t = eidx.shape
    tt = min(t, 2048)
    spec = pl.BlockSpec((k, tt), lambda i, *_: (0, i))
    return pl.pallas_call(
        _slots_kernel,
        out_shape=jax.ShapeDtypeStruct((k, t), jnp.int32),
        grid_spec=pltpu.PrefetchScalarGridSpec(num_scalar_prefetch=1, grid=(t // tt,), in_specs=[spec, spec],
                                               out_specs=spec),
        compiler_params=_cparams(("arbitrary",)),
        name="slots",
    )(start, eidx, rank)


def _dest_copy(t, td, dest_hbm, d_s, sem):
    return pltpu.make_async_copy(dest_hbm.at[:, pl.ds(t * td, td)], d_s.at[t % 2], sem.at[t % 2])


def _dispatch_kernel(h2p_hbm, dest_hbm, xs_hbm, hbuf, d_s, fsem, rsem, dsem):
    i = pl.program_id(0)
    n = pl.num_programs(0)
    td = hbuf.shape[1]

    def fetch(t):
        slot = t % 3
        return pltpu.make_async_copy(h2p_hbm.at[pl.ds(t * td, td), :], hbuf.at[slot], fsem.at[slot])

    def wait_rows(t):
        for _ in range(td * TOP_K):
            _row_copy(hbuf.at[0], 0, xs_hbm, 0, dsem.at[t % 2]).wait()

    @pl.when(i == 0)
    def _():
        fetch(0).start()
        _dest_copy(0, td, dest_hbm, d_s, rsem).start()

        @pl.when(n > 1)
        def _():
            fetch(1).start()

    @pl.when(i + 1 < n)
    def _():
        _dest_copy(i + 1, td, dest_hbm, d_s, rsem).start()

    fetch(i).wait()
    _dest_copy(i, td, dest_hbm, d_s, rsem).wait()

    src = hbuf.at[i % 3]
    dst = d_s.at[i % 2]
    sem = dsem.at[i % 2]
    for j in range(td):
        for k in range(TOP_K):
            _row_copy(src, j, xs_hbm, dst[k, j], sem).start(priority=(j * TOP_K + k) % 2)

    @pl.when(i >= 1)
    def _():
        wait_rows(i - 1)

    @pl.when(i + 2 < n)
    def _():
        fetch(i + 2).start()

    @pl.when(i == n - 1)
    def _():
        wait_rows(i)


def _dispatch(h2p, dest, td):
    t, dw = h2p.shape
    any_spec = pl.BlockSpec(memory_space=pl.ANY)
    return pl.pallas_call(
        _dispatch_kernel,
        out_shape=jax.ShapeDtypeStruct((t * TOP_K, dw), jnp.uint32),
        grid=(t // td,),
        in_specs=[any_spec, any_spec],
        out_specs=any_spec,
        scratch_shapes=[pltpu.VMEM((3, td, dw), jnp.uint32), pltpu.SMEM((2, TOP_K, td), jnp.int32),
                        pltpu.SemaphoreType.DMA((3,)), pltpu.SemaphoreType.DMA((2,)),
                        pltpu.SemaphoreType.DMA((2,))],
        compiler_params=_cparams(("arbitrary",)),
        name="dispatch",
    )(h2p, dest)


def _experts_kernel(first_ref, e_ref, lo_ref, hi_ref, slot_ref, next_ref, xs_ref, w1_hbm, w3_hbm, w2_hbm, ys_ref,
                    w1_f, w3_f, w2_f, w13_b, w2_b, wsem):
    b = pl.program_id(0)
    bm, dw = xs_ref.shape
    ff = w1_hbm.shape[2]
    base = b * bm

    def fetch(e, slot):
        return (pltpu.make_async_copy(w1_hbm.at[e], w1_f.at[slot], wsem.at[slot, 0]),
                pltpu.make_async_copy(w3_hbm.at[e], w3_f.at[slot], wsem.at[slot, 1]),
                pltpu.make_async_copy(w2_hbm.at[e], w2_f.at[slot], wsem.at[slot, 2]))

    def swiglu():
        x_lo, x_hi = _unpack_pair(xs_ref[...])
        gu = _dot(x_lo.astype(BF16), w13_b[:dw, :]) + _dot(x_hi.astype(BF16), w13_b[dw:, :])
        g = gu[:, :ff]
        act = g * jax.nn.sigmoid(g) * gu[:, ff:]
        y = _dot(act.astype(BF16), w2_b[...])
        return _pack_pair(y[:, :dw], y[:, dw:])

    def item(j, carry):
        lo = lo_ref[j]
        hi = hi_ref[j]

        @pl.when(j == 0)
        def _():
            for cp in fetch(e_ref[0], 0):
                cp.start()

        @pl.when(jnp.logical_or(j == 0, e_ref[j] != e_ref[jnp.maximum(j - 1, 0)]))
        def _():
            slot = slot_ref[j]
            for cp in fetch(e_ref[j], slot):
                cp.wait()

            @pl.when(next_ref[j] >= 0)
            def _():
                for cp in fetch(next_ref[j], 1 - slot):
                    cp.start()

            w13_b[:, :ff] = w1_f[slot].astype(BF16)
            w13_b[:, ff:] = w3_f[slot].astype(BF16)
            w2_b[...] = w2_f[slot].astype(BF16)

        @pl.when(jnp.logical_and(hi > lo, lo == base))
        def _():
            ys_ref[...] = swiglu()

        @pl.when(jnp.logical_and(hi > lo, lo != base))
        def _():
            rows = base + lax.broadcasted_iota(jnp.int32, (bm, 1), 0)
            ys_ref[...] = jnp.where(rows >= lo, swiglu(), ys_ref[...])

        return carry

    lax.fori_loop(first_ref[b], first_ref[b + 1], item, 0)


def _experts(items, xs, w1, w3, w2, bm):
    n_slots, dw = xs.shape
    _, d, ff = w1.shape
    any_spec = pl.BlockSpec(memory_space=pl.ANY)
    rows = pl.BlockSpec((bm, dw), lambda b, *_: (b, 0))
    return pl.pallas_call(
        _experts_kernel,
        out_shape=jax.ShapeDtypeStruct((n_slots, dw), jnp.uint32),
        grid_spec=pltpu.PrefetchScalarGridSpec(
            num_scalar_prefetch=len(items),
            grid=(n_slots // bm,),
            in_specs=[rows, any_spec, any_spec, any_spec],
            out_specs=rows,
            scratch_shapes=[pltpu.VMEM((2, d, ff), F32), pltpu.VMEM((2, d, ff), F32), pltpu.VMEM((2, ff, d), F32),
                            pltpu.VMEM((d, 2 * ff), BF16), pltpu.VMEM((ff, d), BF16),
                            pltpu.SemaphoreType.DMA((2, 3))]),
        compiler_params=_cparams(("arbitrary",)),
        name="experts",
    )(*items, xs, w1, w3, w2)


def _combine_kernel(d_model, tiles_per_batch, n_tiles, xp_ref, wts_ref, mod_ref, g_final_ref, dest_hbm, ys_hbm,
                    o_ref, gbuf0, gbuf1, d_s, rsem, gsem):
    s = pl.program_id(0)
    tc = xp_ref.shape[0]
    dw = d_model // 2
    gbufs = (gbuf0, gbuf1)
    n_parts = TOP_K

    @pl.when(s == 0)
    def _():
        _dest_copy(0, tc, dest_hbm, d_s, rsem).start()

    @pl.when(s + 1 < n_tiles)
    def _():
        _dest_copy(s + 1, tc, dest_hbm, d_s, rsem).start()

    def step(par, do_issue, do_reduce):
        cur, prev = gbufs[par], gbufs[1 - par]
        if do_issue:
            _dest_copy(s, tc, dest_hbm, d_s, rsem).wait()
        if do_reduce:
            for _ in range(tc * TOP_K):
                _row_copy(ys_hbm, 0, prev.at[0], 0, gsem.at[1 - par]).wait()
            w = wts_ref[...]
        r_lo = r_hi = None
        for part in range(n_parts):
            if do_issue:
                for j in range(part * tc // n_parts, (part + 1) * tc // n_parts):
                    for k in range(TOP_K):
                        _row_copy(ys_hbm, d_s[par, k, j], cur.at[k], j,
                                  gsem.at[par]).start(priority=(j * TOP_K + k) % 2)
            if do_reduce:
                y_lo, y_hi = _unpack_pair(prev[part])
                wk = w[:, part:part + 1]
                r_lo = wk * y_lo if r_lo is None else r_lo + wk * y_lo
                r_hi = wk * y_hi if r_hi is None else r_hi + wk * y_hi
        if do_reduce:
            b = (s - 1) // tiles_per_batch
            gt2 = mod_ref[pl.ds(b, 1), 5 * d_model:6 * d_model]
            v_lo = xp_ref[:, :dw] + gt2[:, :dw] * r_lo
            v_hi = xp_ref[:, dw:] + gt2[:, dw:] * r_hi
            ms = (jnp.sum(v_lo * v_lo, axis=-1, keepdims=True)
                  + jnp.sum(v_hi * v_hi, axis=-1, keepdims=True)) / d_model
            inv = lax.rsqrt(ms + NORM_EPS)
            o_ref[:, :dw] = v_lo * inv * g_final_ref[:, :dw]
            o_ref[:, dw:] = v_hi * inv * g_final_ref[:, dw:]

    @pl.when(s == 0)
    def _():
        step(0, True, False)

    @pl.when(jnp.logical_and(jnp.logical_and(s >= 1, s < n_tiles), s % 2 == 0))
    def _():
        step(0, True, True)

    @pl.when(jnp.logical_and(s < n_tiles, s % 2 == 1))
    def _():
        step(1, True, True)

    @pl.when(s == n_tiles)
    def _():
        step(n_tiles % 2, False, True)


def _combine(xp, wts_t, mod, g_final, dest, ys, n_lat, tc):
    t, d = xp.shape
    any_spec = pl.BlockSpec(memory_space=pl.ANY)
    prev = lambda s: (jnp.maximum(s - 1, 0), 0)
    gbuf = pltpu.VMEM((TOP_K, tc, d // 2), jnp.uint32)
    return pl.pallas_call(
        functools.partial(_combine_kernel, d, n_lat // tc, t // tc),
        out_shape=jax.ShapeDtypeStruct((t, d), F32),
        grid=(t // tc + 1,),
        in_specs=[pl.BlockSpec((tc, d), prev), pl.BlockSpec((tc, TOP_K), prev),
                  pl.BlockSpec(mod.shape, lambda s: (0, 0)), pl.BlockSpec(g_final.shape, lambda s: (0, 0)),
                  any_spec, any_spec],
        out_specs=pl.BlockSpec((tc, d), prev),
        scratch_shapes=[gbuf, gbuf, pltpu.SMEM((2, TOP_K, tc), jnp.int32),
                        pltpu.SemaphoreType.DMA((2,)), pltpu.SemaphoreType.DMA((2,))],
        compiler_params=_cparams(("arbitrary",)),
        name="combine",
    )(xp, wts_t, mod, g_final, dest, ys)


def _swap_halves(w, width):
    r, c = w.shape
    return w.reshape(r, c // width, 2, width // 2)[:, :, ::-1, :].reshape(r, c)


def _pad_cols(w, left, total):
    return jnp.pad(w, ((0, 0), (left, total - left - w.shape[1])))


def _layout_weights(w_in, w_uq, w_ukv):
    s0 = MLA_Q_RANK
    s1 = s0 + MLA_KV_RANK
    s2 = s1 + MLA_ROPE
    s3 = s2 + DIFF_HEADS * 2 * DIFF_QK
    s4 = s3 + DIFF_HEADS * 2 * DIFF_QK
    log2e = 1.0 / math.log(2.0)
    w_kr, w_dq, w_dk = w_in[:, s1:s2], w_in[:, s2:s3] * (log2e / math.sqrt(DIFF_QK)), w_in[:, s3:s4]
    w_uq = w_uq * (log2e / math.sqrt(MLA_NOPE + MLA_ROPE))
    w_in_ext = jnp.concatenate([
        w_in[:, :s1], w_dq, _swap_halves(w_dq, DIFF_QK), w_dk, _swap_halves(w_dk, DIFF_QK), w_in[:, s4:],
        _pad_cols(w_kr, MLA_NOPE, LANES), _pad_cols(_swap_halves(w_kr, MLA_ROPE), MLA_NOPE, LANES)],
        axis=1).astype(BF16)

    qk = MLA_NOPE + MLA_ROPE
    qa, qb, kk, vv = [], [], [], []
    for h in range(MLA_HEADS):
        wq = w_uq[:, h * qk:(h + 1) * qk]
        qa.append(_pad_cols(wq, 0, LANES))
        qb.append(_pad_cols(_swap_halves(wq[:, MLA_NOPE:], MLA_ROPE), MLA_NOPE, LANES))
        wkv = w_ukv[:, h * (MLA_NOPE + MLA_V):(h + 1) * (MLA_NOPE + MLA_V)]
        kk.append(_pad_cols(wkv[:, :MLA_NOPE], 0, LANES))
        vv.append(wkv[:, MLA_NOPE:])
    w_uq_ext = jnp.concatenate(qa + qb, axis=1).astype(BF16)
    w_ukv_ext = jnp.concatenate(kk + vv, axis=1).astype(BF16)
    return w_in_ext, w_uq_ext, w_ukv_ext


def _rotary_tables(n_ctx, n_lat):
    def angles(rot_dim):
        n_freq = rot_dim // 4
        inv = ROPE_BASE ** (-(np.arange(n_freq, dtype=np.float64) / n_freq))
        rows = n_lat // GRID_W
        row = np.repeat(np.arange(rows, dtype=np.float64), GRID_W)
        col = np.tile(np.arange(GRID_W, dtype=np.float64), rows)
        theta = np.concatenate([row[:, None] * inv, col[:, None] * inv], axis=-1)
        theta = np.concatenate([np.zeros((n_ctx, 2 * n_freq)), theta], axis=0)
        return np.cos(theta), np.sin(theta)

    n = n_ctx + n_lat
    cm, sm = angles(MLA_ROPE)
    cd, sd = angles(DIFF_QK)
    pad_m = LANES - MLA_NOPE - MLA_ROPE
    cq = np.concatenate([np.ones((n, MLA_NOPE)), cm, cm, np.zeros((n, pad_m))], axis=1)
    sq = np.concatenate([np.zeros((n, MLA_NOPE)), -sm, sm, np.zeros((n, pad_m))], axis=1)
    cdd = np.concatenate([cd, cd, cd, cd], axis=1)
    sdd = np.concatenate([-sd, sd, -sd, sd], axis=1)
    return jnp.asarray(np.concatenate([cq, sq, cdd, sdd], axis=1), F32)


def kernel(x, c, ctx, c_ctx, w_mod, b_mod, g_attn, g_ffn, w_in, g_q_lat, w_uq, g_kv_lat, w_ukv, lam_q1, lam_k1,
           lam_q2, lam_k2, g_subln, w_out, w_router, router_bias, w1, w3, w2, ws1, ws3, ws2, g_final):
    n_batch, n_lat, d = x.shape
    n_ctx = ctx.shape[1]
    t = n_batch * n_lat
    tm = 256
    tq = 512
    bm = 256
    td = 128
    assert w_mod.shape[0] == 1 and n_ctx % tm == 0 and n_lat % tq == 0 and n_lat % tm == 0 and t % td == 0
    assert (t * TOP_K) % bm == 0

    cc = jnp.concatenate([c, c_ctx[None, :], jnp.zeros((8 - n_batch - 1, d), F32)], axis=0)
    mod = _modulation(cc, w_mod[0], b_mod)

    w_in_ext, w_uq_ext, w_ukv_ext = _layout_weights(w_in[0], w_uq[0], w_ukv[0])
    tables = _rotary_tables(n_ctx, n_lat)
    qm, km, vmt, qd, k12, vdt = _projections(ctx, x, mod, g_attn, w_in_ext, g_q_lat, w_uq_ext, g_kv_lat,
                                             w_ukv_ext, tables, tm)
    om = _attn_mla(qm, km, vmt, tq)
    od = _attn_diff((lam_q1, lam_k1, lam_q2, lam_k2), g_subln.reshape(-1, 1), qd, k12, vdt, tq)

    ws13 = jnp.concatenate([ws1[0], ws3[0]], axis=1).astype(BF16)
    xp, h2p, eidx, rank, wts, cnt = _post(
        om.reshape(t, -1), od.reshape(t, -1), x.reshape(t, d), mod, w_out[0].astype(BF16), g_ffn, ws13,
        ws2[0].astype(BF16), w_router[0].T, router_bias[0][:, None], n_lat, tm)

    counts = cnt[:, 0]
    start = (jnp.cumsum(counts) - counts).astype(jnp.int32)
    n_slots = t * TOP_K
    n_blocks = n_slots // bm
    blk_b = jnp.arange(n_blocks, dtype=jnp.int32) * bm
    exp_b = jnp.concatenate([start[1:], jnp.full((1,), n_slots, jnp.int32)])
    pos_blk = jnp.arange(n_blocks, dtype=jnp.int32) + jnp.sum(exp_b[None, :] < blk_b[:, None], axis=1)
    pos_exp = jnp.arange(N_EXPERTS, dtype=jnp.int32) + jnp.sum(blk_b[None, :] <= exp_b[:, None], axis=1)
    slot = jnp.arange(n_blocks + N_EXPERTS, dtype=jnp.int32)[:, None]
    bounds = (jnp.sum(jnp.where(pos_blk[None, :] == slot, blk_b[None, :], 0), axis=1)
              + jnp.sum(jnp.where(pos_exp[None, :] == slot, exp_b[None, :], 0), axis=1))
    item_lo, item_hi = bounds[:-1], bounds[1:]
    item_blk = jnp.minimum(item_lo // bm, n_blocks - 1)
    item_e = jnp.clip(jnp.sum(start[None, :] <= item_lo[:, None], axis=1) - 1, 0, N_EXPERTS - 1).astype(jnp.int32)
    changed = jnp.concatenate([jnp.zeros((1,), jnp.int32), (item_e[1:] != item_e[:-1]).astype(jnp.int32)])
    item_slot = jnp.cumsum(changed).astype(jnp.int32) % 2
    later = jnp.where(item_e[None, :] > item_e[:, None], item_e[None, :], N_EXPERTS)
    item_next = jnp.min(later, axis=1)
    item_next = jnp.where(item_next == N_EXPERTS, -1, item_next).astype(jnp.int32)

    dest = _slots(start, eidx, rank)
    xs = _dispatch(h2p, dest, td)
    blk_ids = jnp.arange(n_blocks + 1, dtype=jnp.int32)
    item_first = jnp.sum(item_blk[None, :] < blk_ids[:, None], axis=1).astype(jnp.int32)
    ys = _experts((item_first, item_e, item_lo, item_hi, item_slot, item_next), xs, w1[0], w3[0], w2[0], bm)
    out = _combine(xp, wts.T, mod, g_final[None, :], dest, ys, n_lat, td)
    return out.reshape(n_batch, n_lat, d)
```

```python
import functools
import math

import jax
import jax.numpy as jnp
import numpy as np
from jax import lax
from jax.experimental import pallas as pl
from jax.experimental.pallas import tpu as pltpu

GRID_W = 64
ROPE_BASE = 10000.0
NORM_EPS = 1e-6
MLA_HEADS = 8
MLA_NOPE = 64
MLA_ROPE = 32
MLA_V = 64
MLA_Q_RANK = 256
MLA_KV_RANK = 128
DIFF_HEADS = 4
DIFF_QK = 64
DIFF_V = 2 * DIFF_QK
N_EXPERTS = 256
TOP_K = 8
N_GROUPS = 8
TOPK_GROUPS = 4
ROUTED_SCALE = 2.5
LAM_INIT = 0.8 - 0.6 * math.exp(-0.3 * 0)

LANES = 128
VMEM_LIMIT = 48 * 1024 * 1024

F32 = jnp.float32
BF16 = jnp.bfloat16
NEG_INF = float("-inf")


def _cparams(sem):
    return pltpu.CompilerParams(dimension_semantics=sem, vmem_limit_bytes=VMEM_LIMIT)


def _rms_rows(x, g):
    return x * lax.rsqrt(jnp.mean(x * x, axis=-1, keepdims=True) + NORM_EPS) * g


def _dot(a, b):
    return jnp.dot(a, b, preferred_element_type=F32)


def _dot_nt(a, b):
    return lax.dot_general(a, b, (((1,), (1,)), ((), ())), preferred_element_type=F32)


def _mod_kernel(c_ref, w_ref, b_ref, o_ref):
    a = c_ref[...]
    a = a * jax.nn.sigmoid(a)
    o_ref[...] = jnp.dot(a, w_ref[...], preferred_element_type=F32,
                         precision=lax.Precision.HIGHEST) + b_ref[...]


def _modulation(cc, w_mod, b_mod):
    rows, d = cc.shape
    cols = w_mod.shape[1]
    tn = 1536
    return pl.pallas_call(
        _mod_kernel,
        out_shape=jax.ShapeDtypeStruct((rows, cols), F32),
        grid=(cols // tn,),
        in_specs=[pl.BlockSpec((rows, d), lambda j: (0, 0)),
                  pl.BlockSpec((d, tn), lambda j: (0, j)),
                  pl.BlockSpec((1, tn), lambda j: (0, j))],
        out_specs=pl.BlockSpec((rows, tn), lambda j: (0, j)),
        compiler_params=_cparams(("arbitrary",)),
        name="mod",
    )(cc, w_mod, b_mod)


_C_Q = 0
_C_KV = _C_Q + MLA_Q_RANK
_C_DQ = _C_KV + MLA_KV_RANK
_C_DQS = _C_DQ + DIFF_HEADS * 2 * DIFF_QK
_C_DK = _C_DQS + DIFF_HEADS * 2 * DIFF_QK
_C_DKS = _C_DK + DIFF_HEADS * 2 * DIFF_QK
_C_DV = _C_DKS + DIFF_HEADS * 2 * DIFF_QK
_C_KRA = _C_DV + DIFF_HEADS * DIFF_V
_C_KRB = _C_KRA + LANES
_IN_EXT = _C_KRB + LANES
_N_TAB = 4


def _proj_kernel(n_batch, d_model, ctx_ref, x_ref, mod_ref, g_attn_ref, w_in_ref, g_q_ref, w_uq_ref,
                 g_kv_ref, w_ukv_ref, tab_ref,
                 qm_ref, km_ref, vmt_ref, qd_ref, k12_ref, vdt_ref, *, n_ctx_tiles):
    b = pl.program_id(0)
    i = pl.program_id(1)
    is_ctx = i < n_ctx_tiles
    xin = jnp.where(is_ctx, ctx_ref[0], x_ref[0])
    row = jnp.where(is_ctx, n_batch, b)
    sh1 = mod_ref[pl.ds(row, 1), 0:d_model]
    sc1 = mod_ref[pl.ds(row, 1), d_model:2 * d_model]
    h = _rms_rows(xin, g_attn_ref[...]) * (1.0 + sc1) + sh1
    p = _dot(h.astype(BF16), w_in_ref[...])

    cq = _rms_rows(p[:, _C_Q:_C_Q + MLA_Q_RANK], g_q_ref[...])
    qa = _dot(cq.astype(BF16), w_uq_ref[...])
    ckv = _rms_rows(p[:, _C_KV:_C_KV + MLA_KV_RANK], g_kv_ref[...])
    kv = _dot(ckv.astype(BF16), w_ukv_ref[...])

    def tab(j):
        return tab_ref[:, j * LANES:(j + 1) * LANES]

    kr = p[:, _C_KRA:_C_KRA + LANES] * tab(0) + p[:, _C_KRB:_C_KRB + LANES] * tab(1)
    first_half = lax.broadcasted_iota(jnp.int32, (1, LANES), 1) < DIFF_QK
    hw = MLA_HEADS * LANES
    for hd in range(MLA_HEADS):
        lo = hd * LANES
        qm_ref[0, hd] = (qa[:, lo:lo + LANES] * tab(0) + qa[:, hw + lo:hw + lo + LANES] * tab(1)).astype(BF16)
        km_ref[0, hd] = (kv[:, lo:lo + LANES] + kr).astype(BF16)
    for pr in range(MLA_HEADS // 2):
        lo = hw + pr * LANES
        vmt_ref[0, pr] = kv[:, lo:lo + LANES].T.astype(BF16)
    for hd in range(DIFF_HEADS):
        lo = hd * LANES
        dq = p[:, _C_DQ + lo:_C_DQ + lo + LANES]
        dqs = p[:, _C_DQS + lo:_C_DQS + lo + LANES]
        qr = dq * tab(2) + dqs * tab(3)
        qd_ref[0, hd, 0] = jnp.where(first_half, qr, 0.0).astype(BF16)
        qd_ref[0, hd, 1] = jnp.where(first_half, 0.0, qr).astype(BF16)
        dk = p[:, _C_DK + lo:_C_DK + lo + LANES]
        dks = p[:, _C_DKS + lo:_C_DKS + lo + LANES]
        k12_ref[0, hd] = (dk * tab(2) + dks * tab(3)).astype(BF16)
        vdt_ref[0, hd] = p[:, _C_DV + lo:_C_DV + lo + LANES].T.astype(BF16)


def _projections(ctx, x, mod, g_attn, w_in_ext, g_q, w_uq_ext, g_kv, w_ukv_ext, tables, tm):
    n_batch, n_ctx, d = ctx.shape
    n_lat = x.shape[1]
    n_all = n_ctx + n_lat
    nct = n_ctx // tm
    nt = n_all // tm

    def full(a):
        return pl.BlockSpec(a.shape, lambda b, i: (0,) * a.ndim)

    def k_spec(heads):
        return pl.BlockSpec((1, heads, tm, LANES), lambda b, i: (b, 0, i, 0))

    def vt_spec(heads):
        return pl.BlockSpec((1, heads, LANES, tm), lambda b, i: (b, 0, 0, i))

    def lat(i):
        return jnp.maximum(i - nct, 0)

    def sds(*shape):
        return jax.ShapeDtypeStruct((n_batch,) + shape, BF16)

    return pl.pallas_call(
        functools.partial(_proj_kernel, n_batch, d, n_ctx_tiles=nct),
        out_shape=(sds(MLA_HEADS, n_lat, LANES), sds(MLA_HEADS, n_all, LANES), sds(MLA_HEADS // 2, LANES, n_all),
                   sds(DIFF_HEADS, 2, n_lat, LANES), sds(DIFF_HEADS, n_all, LANES),
                   sds(DIFF_HEADS, LANES, n_all)),
        grid=(n_batch, nt),
        in_specs=[pl.BlockSpec((1, tm, d), lambda b, i: (b, jnp.minimum(i, nct - 1), 0)),
                  pl.BlockSpec((1, tm, d), lambda b, i: (b, jnp.maximum(i - nct, 0), 0)),
                  full(mod), full(g_attn), full(w_in_ext), full(g_q), full(w_uq_ext), full(g_kv),
                  full(w_ukv_ext),
                  pl.BlockSpec((tm, _N_TAB * LANES), lambda b, i: (i, 0))],
        out_specs=(pl.BlockSpec((1, MLA_HEADS, tm, LANES), lambda b, i: (b, 0, lat(i), 0)),
                   k_spec(MLA_HEADS), vt_spec(MLA_HEADS // 2),
                   pl.BlockSpec((1, DIFF_HEADS, 2, tm, LANES), lambda b, i: (b, 0, 0, lat(i), 0)),
                   k_spec(DIFF_HEADS), vt_spec(DIFF_HEADS)),
        compiler_params=_cparams(("arbitrary", "arbitrary")),
        name="proj",
    )(ctx, x, mod, g_attn, w_in_ext, g_q, w_uq_ext, g_kv, w_ukv_ext, tables)


def _attend_t(q, k, vt):
    st = _dot_nt(k, q)
    et = jnp.exp2(st - jnp.max(st, axis=0, keepdims=True))
    l = jnp.sum(et, axis=0, keepdims=True)
    return _dot(vt, et.astype(BF16)) * (1.0 / l)


def _attn_mla_kernel(q_ref, k_ref, vt_ref, o_ref):
    sts = [_dot_nt(k_ref[0, j], q_ref[0, j]) for j in range(2)]
    ets = [jnp.exp2(st - jnp.max(st, axis=0, keepdims=True)) for st in sts]
    ls = [jnp.sum(et, axis=0, keepdims=True) for et in ets]
    outs = [_dot(vt_ref[0, 0, j * MLA_V:(j + 1) * MLA_V, :], ets[j].astype(BF16)) * (1.0 / ls[j]) for j in range(2)]
    o_ref[0] = jnp.concatenate(outs, axis=0).T.astype(BF16)


def _attn_mla(qm, km, vmt, tq):
    n_batch, heads, n_lat, _ = qm.shape
    n_all = km.shape[2]
    return pl.pallas_call(
        _attn_mla_kernel,
        out_shape=jax.ShapeDtypeStruct((n_batch, n_lat, heads // 2 * LANES), BF16),
        grid=(n_batch, heads // 2, n_lat // tq),
        in_specs=[pl.BlockSpec((1, 2, tq, LANES), lambda b, h, i: (b, h, i, 0)),
                  pl.BlockSpec((1, 2, n_all, LANES), lambda b, h, i: (b, h, 0, 0)),
                  pl.BlockSpec((1, 1, LANES, n_all), lambda b, h, i: (b, h, 0, 0))],
        out_specs=pl.BlockSpec((1, tq, LANES), lambda b, h, i: (b, i, h)),
        compiler_params=_cparams(("arbitrary", "arbitrary", "arbitrary")),
        name="attn_m",
    )(qm, km, vmt)


def _attn_diff_kernel(lq1_ref, lk1_ref, lq2_ref, lk2_ref, g_sub_ref, q_ref, k_ref, vt_ref, o_ref):
    lam = (jnp.exp(jnp.sum(lq1_ref[...] * lk1_ref[...], axis=-1, keepdims=True))
           - jnp.exp(jnp.sum(lq2_ref[...] * lk2_ref[...], axis=-1, keepdims=True)) + LAM_INIT)
    tq = q_ref.shape[3]
    o = _attend_t(q_ref[0, 0].reshape(2 * tq, LANES), k_ref[0, 0], vt_ref[0, 0])
    ot = o[:, :tq] - lam * o[:, tq:]
    ot = ot * lax.rsqrt(jnp.mean(ot * ot, axis=0, keepdims=True) + NORM_EPS) * g_sub_ref[...]
    o_ref[0] = (ot * (1.0 - LAM_INIT)).T.astype(BF16)


def _attn_diff(lams, g_sub_col, qd, k12, vdt, tq):
    n_batch, heads, _, n_lat, _ = qd.shape
    n_all = k12.shape[2]

    def small(a):
        return pl.BlockSpec(a.shape, lambda b, h, i: (0,) * a.ndim)

    return pl.pallas_call(
        _attn_diff_kernel,
        out_shape=jax.ShapeDtypeStruct((n_batch, n_lat, heads * LANES), BF16),
        grid=(n_batch, heads, n_lat // tq),
        in_specs=[small(lams[0]), small(lams[1]), small(lams[2]), small(lams[3]), small(g_sub_col),
                  pl.BlockSpec((1, 1, 2, tq, LANES), lambda b, h, i: (b, h, 0, i, 0)),
                  pl.BlockSpec((1, 1, n_all, LANES), lambda b, h, i: (b, h, 0, 0)),
                  pl.BlockSpec((1, 1, LANES, n_all), lambda b, h, i: (b, h, 0, 0))],
        out_specs=pl.BlockSpec((1, tq, LANES), lambda b, h, i: (b, i, h)),
        compiler_params=_cparams(("arbitrary", "arbitrary", "arbitrary")),
        name="attn_d",
    )(*lams, g_sub_col, qd, k12, vdt)


def _post_kernel(d_model, tiles_per_batch, om_ref, od_ref, x_ref, mod_ref, w_out_ref, g_ffn_ref, ws13_ref,
                 ws2_ref, wr_hi_ref, wr_lo_ref, br_ref,
                 xp_ref, h2p_ref, eidx_ref, rank_ref, wts_ref, cnt_ref, carry_ref):
    i = pl.program_id(0)
    tm = x_ref.shape[0]
    n_exp = wr_hi_ref.shape[0]
    per_group = n_exp // N_GROUPS

    @pl.when(i == 0)
    def _():
        carry_ref[...] = jnp.zeros_like(carry_ref)

    b = i // tiles_per_batch

    def modv(j):
        return mod_ref[pl.ds(b, 1), j * d_model:(j + 1) * d_model]

    half = om_ref.shape[1]
    y = _dot(om_ref[...], w_out_ref[0:half, :]) + _dot(od_ref[...], w_out_ref[half:, :])
    x1 = x_ref[...] + modv(2) * y
    h2 = _rms_rows(x1, g_ffn_ref[...]) * (1.0 + modv(4)) + modv(3)
    h2p_ref[...] = _pack_pair(h2[:, :d_model // 2], h2[:, d_model // 2:])

    h2b = h2.astype(BF16)
    gu = _dot(h2b, ws13_ref[...])
    ff = gu.shape[1] // 2
    g = gu[:, :ff]
    act = g * jax.nn.sigmoid(g) * gu[:, ff:]
    shared = _dot(act.astype(BF16), ws2_ref[...])
    xp_ref[...] = x1 + modv(5) * shared

    h2_top = lax.bitcast_convert_type(lax.bitcast_convert_type(h2, jnp.uint32) & jnp.uint32(_HI_MASK), F32)
    h2_hi = h2_top.astype(BF16)
    h2_lo = (h2 - h2_top).astype(BF16)
    logits = _dot_nt(wr_hi_ref[...], h2_hi) + (_dot_nt(wr_hi_ref[...], h2_lo)
                                               + _dot_nt(wr_lo_ref[...], h2_hi))
    s = jax.nn.sigmoid(logits)
    ssel = s + br_ref[...]
    s3 = ssel.reshape(N_GROUPS, per_group, tm)
    m1 = jnp.max(s3, axis=1, keepdims=True)
    eq = s3 == m1
    n_eq = jnp.sum(jnp.where(eq, 1.0, 0.0), axis=1, keepdims=True)
    m2 = jnp.max(jnp.where(eq, NEG_INF, s3), axis=1, keepdims=True)
    grp = m1 + jnp.where(n_eq >= 2.0, m1, m2)

    gi = lax.broadcasted_iota(jnp.int32, grp.shape, 0).astype(F32)
    gcur = grp
    gsel = jnp.zeros_like(grp)
    for _ in range(TOPK_GROUPS):
        gm = jnp.max(gcur, axis=0, keepdims=True)
        first = jnp.min(jnp.where(gcur == gm, gi, float(N_GROUPS)), axis=0, keepdims=True)
        oh = gi == first
        gsel = jnp.where(oh, 1.0, gsel)
        gcur = jnp.where(oh, NEG_INF, gcur)
    cur = jnp.where(gsel > 0.0, s3, NEG_INF).reshape(n_exp, tm)

    ie = lax.broadcasted_iota(jnp.int32, (n_exp, tm), 0).astype(F32)
    sel = jnp.zeros((n_exp, tm), F32)
    e_rows = []
    s_rows = []
    for _ in range(TOP_K):
        m = jnp.max(cur, axis=0, keepdims=True)
        first = jnp.min(jnp.where(cur == m, ie, float(n_exp)), axis=0, keepdims=True)
        oh = ie == first
        cur = jnp.where(oh, NEG_INF, cur)
        sel = jnp.where(oh, 1.0, sel)
        e_rows.append(first)
        s_rows.append(jnp.sum(jnp.where(oh, s, 0.0), axis=0, keepdims=True))
    s_tot = s_rows[0]
    for r in s_rows[1:]:
        s_tot = s_tot + r

    tr = lax.broadcasted_iota(jnp.int32, (tm, tm), 0)
    tc = lax.broadcasted_iota(jnp.int32, (tm, tm), 1)
    upper = jnp.where(tr < tc, 1.0, 0.0).astype(BF16)
    rank = _dot(sel.astype(BF16), upper) + carry_ref[...]
    for k in range(TOP_K):
        eidx_ref[k:k + 1, :] = e_rows[k].astype(jnp.int32)
        rank_ref[k:k + 1, :] = jnp.sum(jnp.where(ie == e_rows[k], rank, 0.0), axis=0,
                                       keepdims=True).astype(jnp.int32)
        wts_ref[k:k + 1, :] = s_rows[k] / s_tot * ROUTED_SCALE
    carry_ref[...] = carry_ref[...] + jnp.sum(sel, axis=1, keepdims=True)
    cnt_ref[...] = carry_ref[...].astype(jnp.int32)


def _post(om, od, x2d, mod, w_out, g_ffn, ws13, ws2, wr_t, br, n_lat, tm):
    t, d = x2d.shape
    n_exp = wr_t.shape[0]
    wr_top = lax.bitcast_convert_type(lax.bitcast_convert_type(wr_t, jnp.uint32) & jnp.uint32(_HI_MASK), F32)
    wr_hi = wr_top.astype(BF16)
    wr_lo = (wr_t - wr_top).astype(BF16)

    def full(a):
        return pl.BlockSpec(a.shape, lambda i: (0,) * a.ndim)

    row = lambda w: pl.BlockSpec((tm, w), lambda i: (i, 0))
    col = pl.BlockSpec((TOP_K, tm), lambda i: (0, i))
    return pl.pallas_call(
        functools.partial(_post_kernel, d, n_lat // tm),
        out_shape=(jax.ShapeDtypeStruct((t, d), F32), jax.ShapeDtypeStruct((t, d // 2), jnp.uint32),
                   jax.ShapeDtypeStruct((TOP_K, t), jnp.int32), jax.ShapeDtypeStruct((TOP_K, t), jnp.int32),
                   jax.ShapeDtypeStruct((TOP_K, t), F32), jax.ShapeDtypeStruct((n_exp, 1), jnp.int32)),
        grid=(t // tm,),
        in_specs=[row(om.shape[1]), row(od.shape[1]), row(d), full(mod), full(w_out), full(g_ffn),
                  full(ws13), full(ws2), full(wr_hi), full(wr_lo), full(br)],
        out_specs=(row(d), row(d // 2), col, col, col, pl.BlockSpec((n_exp, 1), lambda i: (0, 0))),
        scratch_shapes=[pltpu.VMEM((n_exp, 1), F32)],
        compiler_params=_cparams(("arbitrary",)),
        name="post",
    )(om, od, x2d, mod, w_out, g_ffn, ws13, ws2, wr_hi, wr_lo, br)


_HI_MASK = 0xFFFF0000


def _pack_pair(a, b):
    lo = lax.bitcast_convert_type(a.astype(BF16).astype(F32), jnp.uint32) >> 16
    hi = lax.bitcast_convert_type(b.astype(BF16).astype(F32), jnp.uint32) & jnp.uint32(_HI_MASK)
    return lo | hi


def _unpack_pair(w):
    return (lax.bitcast_convert_type(w << 16, F32),
            lax.bitcast_convert_type(w & jnp.uint32(_HI_MASK), F32))


def _row_copy(src, src_row, dst, dst_row, sem):
    return pltpu.make_async_copy(src.at[pl.ds(src_row, 1), :], dst.at[pl.ds(dst_row, 1), :], sem)


def _slots_kernel(start_ref, eidx_ref, rank_ref, dest_ref):
    e = eidx_ref[...]

    def body(x, acc):
        return jnp.where(e == x, start_ref[x], acc)

    dest_ref[...] = lax.fori_loop(0, start_ref.shape[0], body, jnp.zeros_like(e), unroll=8) + rank_ref[...]


def _slots(start, eidx, rank):
    k, t = eidx.shape
    tt = min(t, 2048)
    spec = pl.BlockSpec((k, tt), lambda i, *_: (0, i))
    return pl.pallas_call(
        _slots_kernel,
        out_shape=jax.ShapeDtypeStruct((k, t), jnp.int32),
        grid_spec=pltpu.PrefetchScalarGridSpec(num_scalar_prefetch=1, grid=(t // tt,), in_specs=[spec, spec],
                                               out_specs=spec),
        compiler_params=_cparams(("arbitrary",)),
        name="slots",
    )(start, eidx, rank)


def _dest_copy(t, td, dest_hbm, d_s, sem):
    return pltpu.make_async_copy(dest_hbm.at[:, pl.ds(t * td, td)], d_s.at[t % 2], sem.at[t % 2])


def _dispatch_kernel(h2p_hbm, dest_hbm, xs_hbm, hbuf, d_s, fsem, rsem, dsem):
    i = pl.program_id(0)
    n = pl.num_programs(0)
    td = hbuf.shape[1]

    def fetch(t):
        slot = t % 3
        return pltpu.make_async_copy(h2p_hbm.at[pl.ds(t * td, td), :], hbuf.at[slot], fsem.at[slot])

    def wait_rows(t):
        for _ in range(td * TOP_K):
            _row_copy(hbuf.at[0], 0, xs_hbm, 0, dsem.at[t % 2]).wait()

    @pl.when(i == 0)
    def _():
        fetch(0).start()
        _dest_copy(0, td, dest_hbm, d_s, rsem).start()

        @pl.when(n > 1)
        def _():
            fetch(1).start()

    @pl.when(i + 1 < n)
    def _():
        _dest_copy(i + 1, td, dest_hbm, d_s, rsem).start()

    fetch(i).wait()
    _dest_copy(i, td, dest_hbm, d_s, rsem).wait()

    src = hbuf.at[i % 3]
    dst = d_s.at[i % 2]
    sem = dsem.at[i % 2]
    for j in range(td):
        for k in range(TOP_K):
            _row_copy(src, j, xs_hbm, dst[k, j], sem).start(priority=(j * TOP_K + k) % 2)

    @pl.when(i >= 1)
    def _():
        wait_rows(i - 1)

    @pl.when(i + 2 < n)
    def _():
        fetch(i + 2).start()

    @pl.when(i == n - 1)
    def _():
        wait_rows(i)


def _dispatch(h2p, dest, td):
    t, dw = h2p.shape
    any_spec = pl.BlockSpec(memory_space=pl.ANY)
    return pl.pallas_call(
        _dispatch_kernel,
        out_shape=jax.ShapeDtypeStruct((t * TOP_K, dw), jnp.uint32),
        grid=(t // td,),
        in_specs=[any_spec, any_spec],
        out_specs=any_spec,
        scratch_shapes=[pltpu.VMEM((3, td, dw), jnp.uint32), pltpu.SMEM((2, TOP_K, td), jnp.int32),
                        pltpu.SemaphoreType.DMA((3,)), pltpu.SemaphoreType.DMA((2,)),
                        pltpu.SemaphoreType.DMA((2,))],
        compiler_params=_cparams(("arbitrary",)),
        name="dispatch",
    )(h2p, dest)


def _experts_kernel(first_ref, e_ref, lo_ref, hi_ref, slot_ref, next_ref, xs_ref, w1_hbm, w3_hbm, w2_hbm, ys_ref,
                    w1_f, w3_f, w2_f, w13_b, w2_b, wsem):
    b = pl.program_id(0)
    bm, dw = xs_ref.shape
    ff = w1_hbm.shape[2]
    base = b * bm

    def fetch(e, slot):
        return (pltpu.make_async_copy(w1_hbm.at[e], w1_f.at[slot], wsem.at[slot, 0]),
                pltpu.make_async_copy(w3_hbm.at[e], w3_f.at[slot], wsem.at[slot, 1]),
                pltpu.make_async_copy(w2_hbm.at[e], w2_f.at[slot], wsem.at[slot, 2]))

    def swiglu():
        x_lo, x_hi = _unpack_pair(xs_ref[...])
        gu = _dot(x_lo.astype(BF16), w13_b[:dw, :]) + _dot(x_hi.astype(BF16), w13_b[dw:, :])
        g = gu[:, :ff]
        act = g * jax.nn.sigmoid(g) * gu[:, ff:]
        y = _dot(act.astype(BF16), w2_b[...])
        return _pack_pair(y[:, :dw], y[:, dw:])

    def item(j, carry):
        lo = lo_ref[j]
        hi = hi_ref[j]

        @pl.when(j == 0)
        def _():
            for cp in fetch(e_ref[0], 0):
                cp.start()

        @pl.when(jnp.logical_or(j == 0, e_ref[j] != e_ref[jnp.maximum(j - 1, 0)]))
        def _():
            slot = slot_ref[j]
            for cp in fetch(e_ref[j], slot):
                cp.wait()

            @pl.when(next_ref[j] >= 0)
            def _():
                for cp in fetch(next_ref[j], 1 - slot):
                    cp.start(priority=1)

            w13_b[:, :ff] = w1_f[slot].astype(BF16)
            w13_b[:, ff:] = w3_f[slot].astype(BF16)
            w2_b[...] = w2_f[slot].astype(BF16)

        @pl.when(jnp.logical_and(hi > lo, lo == base))
        def _():
            ys_ref[...] = swiglu()

        @pl.when(jnp.logical_and(hi > lo, lo != base))
        def _():
            rows = base + lax.broadcasted_iota(jnp.int32, (bm, 1), 0)
            ys_ref[...] = jnp.where(rows >= lo, swiglu(), ys_ref[...])

        return carry

    lax.fori_loop(first_ref[b], first_ref[b + 1], item, 0)


def _experts(items, xs, w1, w3, w2, bm):
    n_slots, dw = xs.shape
    _, d, ff = w1.shape
    any_spec = pl.BlockSpec(memory_space=pl.ANY)
    rows = pl.BlockSpec((bm, dw), lambda b, *_: (b, 0))
    return pl.pallas_call(
        _experts_kernel,
        out_shape=jax.ShapeDtypeStruct((n_slots, dw), jnp.uint32),
        grid_spec=pltpu.PrefetchScalarGridSpec(
            num_scalar_prefetch=len(items),
            grid=(n_slots // bm,),
            in_specs=[rows, any_spec, any_spec, any_spec],
            out_specs=rows,
            scratch_shapes=[pltpu.VMEM((2, d, ff), F32), pltpu.VMEM((2, d, ff), F32), pltpu.VMEM((2, ff, d), F32),
                            pltpu.VMEM((d, 2 * ff), BF16), pltpu.VMEM((ff, d), BF16),
                            pltpu.SemaphoreType.DMA((2, 3))]),
        compiler_params=_cparams(("arbitrary",)),
        name="experts",
    )(*items, xs, w1, w3, w2)


def _combine_kernel(d_model, tiles_per_batch, n_tiles, xp_ref, wts_ref, mod_ref, g_final_ref, dest_hbm, ys_hbm,
                    o_ref, gbuf0, gbuf1, d_s, rsem, gsem):
    s = pl.program_id(0)
    tc = xp_ref.shape[0]
    dw = d_model // 2
    gbufs = (gbuf0, gbuf1)
    n_parts = TOP_K

    @pl.when(s == 0)
    def _():
        _dest_copy(0, tc, dest_hbm, d_s, rsem).start()

    @pl.when(s + 1 < n_tiles)
    def _():
        _dest_copy(s + 1, tc, dest_hbm, d_s, rsem).start()

    def step(par, do_issue, do_reduce):
        cur, prev = gbufs[par], gbufs[1 - par]
        if do_issue:
            _dest_copy(s, tc, dest_hbm, d_s, rsem).wait()
        if do_reduce:
            for _ in range(tc * TOP_K):
                _row_copy(ys_hbm, 0, prev.at[0], 0, gsem.at[1 - par]).wait()
            w = wts_ref[...]
        r_lo = r_hi = None
        for part in range(n_parts):
            if do_issue:
                for j in range(part * tc // n_parts, (part + 1) * tc // n_parts):
                    for k in range(TOP_K):
                        _row_copy(ys_hbm, d_s[par, k, j], cur.at[k], j,
                                  gsem.at[par]).start(priority=(j * TOP_K + k) % 2)
            if do_reduce:
                y_lo, y_hi = _unpack_pair(prev[part])
                wk = w[:, part:part + 1]
                r_lo = wk * y_lo if r_lo is None else r_lo + wk * y_lo
                r_hi = wk * y_hi if r_hi is None else r_hi + wk * y_hi
        if do_reduce:
            b = (s - 1) // tiles_per_batch
            gt2 = mod_ref[pl.ds(b, 1), 5 * d_model:6 * d_model]
            v_lo = xp_ref[:, :dw] + gt2[:, :dw] * r_lo
            v_hi = xp_ref[:, dw:] + gt2[:, dw:] * r_hi
            ms = (jnp.sum(v_lo * v_lo, axis=-1, keepdims=True)
                  + jnp.sum(v_hi * v_hi, axis=-1, keepdims=True)) / d_model
            inv = lax.rsqrt(ms + NORM_EPS)
            o_ref[:, :dw] = v_lo * inv * g_final_ref[:, :dw]
            o_ref[:, dw:] = v_hi * inv * g_final_ref[:, dw:]

    @pl.when(s == 0)
    def _():
        step(0, True, False)

    @pl.when(jnp.logical_and(jnp.logical_and(s >= 1, s < n_tiles), s % 2 == 0))
    def _():
        step(0, True, True)

    @pl.when(jnp.logical_and(s < n_tiles, s % 2 == 1))
    def _():
        step(1, True, True)

    @pl.when(s == n_tiles)
    def _():
        step(n_tiles % 2, False, True)


def _combine(xp, wts_t, mod, g_final, dest, ys, n_lat, tc):
    t, d = xp.shape
    any_spec = pl.BlockSpec(memory_space=pl.ANY)
    prev = lambda s: (jnp.maximum(s - 1, 0), 0)
    gbuf = pltpu.VMEM((TOP_K, tc, d // 2), jnp.uint32)
    return pl.pallas_call(
        functools.partial(_combine_kernel, d, n_lat // tc, t // tc),
        out_shape=jax.ShapeDtypeStruct((t, d), F32),
        grid=(t // tc + 1,),
        in_specs=[pl.BlockSpec((tc, d), prev), pl.BlockSpec((tc, TOP_K), prev),
                  pl.BlockSpec(mod.shape, lambda s: (0, 0)), pl.BlockSpec(g_final.shape, lambda s: (0, 0)),
                  any_spec, any_spec],
        out_specs=pl.BlockSpec((tc, d), prev),
        scratch_shapes=[gbuf, gbuf, pltpu.SMEM((2, TOP_K, tc), jnp.int32),
                        pltpu.SemaphoreType.DMA((2,)), pltpu.SemaphoreType.DMA((2,))],
        compiler_params=_cparams(("arbitrary",)),
        name="combine",
    )(xp, wts_t, mod, g_final, dest, ys)


def _swap_halves(w, width):
    r, c = w.shape
    return w.reshape(r, c // width, 2, width // 2)[:, :, ::-1, :].reshape(r, c)


def _pad_cols(w, left, total):
    return jnp.pad(w, ((0, 0), (left, total - left - w.shape[1])))


def _layout_weights(w_in, w_uq, w_ukv):
    s0 = MLA_Q_RANK
    s1 = s0 + MLA_KV_RANK
    s2 = s1 + MLA_ROPE
    s3 = s2 + DIFF_HEADS * 2 * DIFF_QK
    s4 = s3 + DIFF_HEADS * 2 * DIFF_QK
    log2e = 1.0 / math.log(2.0)
    w_kr, w_dq, w_dk = w_in[:, s1:s2], w_in[:, s2:s3] * (log2e / math.sqrt(DIFF_QK)), w_in[:, s3:s4]
    w_uq = w_uq * (log2e / math.sqrt(MLA_NOPE + MLA_ROPE))
    w_in_ext = jnp.concatenate([
        w_in[:, :s1], w_dq, _swap_halves(w_dq, DIFF_QK), w_dk, _swap_halves(w_dk, DIFF_QK), w_in[:, s4:],
        _pad_cols(w_kr, MLA_NOPE, LANES), _pad_cols(_swap_halves(w_kr, MLA_ROPE), MLA_NOPE, LANES)],
        axis=1).astype(BF16)

    qk = MLA_NOPE + MLA_ROPE
    qa, qb, kk, vv = [], [], [], []
    for h in range(MLA_HEADS):
        wq = w_uq[:, h * qk:(h + 1) * qk]
        qa.append(_pad_cols(wq, 0, LANES))
        qb.append(_pad_cols(_swap_halves(wq[:, MLA_NOPE:], MLA_ROPE), MLA_NOPE, LANES))
        wkv = w_ukv[:, h * (MLA_NOPE + MLA_V):(h + 1) * (MLA_NOPE + MLA_V)]
        kk.append(_pad_cols(wkv[:, :MLA_NOPE], 0, LANES))
        vv.append(wkv[:, MLA_NOPE:])
    w_uq_ext = jnp.concatenate(qa + qb, axis=1).astype(BF16)
    w_ukv_ext = jnp.concatenate(kk + vv, axis=1).astype(BF16)
    return w_in_ext, w_uq_ext, w_ukv_ext


def _rotary_tables(n_ctx, n_lat):
    def angles(rot_dim):
        n_freq = rot_dim // 4
        inv = ROPE_BASE ** (-(np.arange(n_freq, dtype=np.float64) / n_freq))
        rows = n_lat // GRID_W
        row = np.repeat(np.arange(rows, dtype=np.float64), GRID_W)
        col = np.tile(np.arange(GRID_W, dtype=np.float64), rows)
        theta = np.concatenate([row[:, None] * inv, col[:, None] * inv], axis=-1)
        theta = np.concatenate([np.zeros((n_ctx, 2 * n_freq)), theta], axis=0)
        return np.cos(theta), np.sin(theta)

    n = n_ctx + n_lat
    cm, sm = angles(MLA_ROPE)
    cd, sd = angles(DIFF_QK)
    pad_m = LANES - MLA_NOPE - MLA_ROPE
    cq = np.concatenate([np.ones((n, MLA_NOPE)), cm, cm, np.zeros((n, pad_m))], axis=1)
    sq = np.concatenate([np.zeros((n, MLA_NOPE)), -sm, sm, np.zeros((n, pad_m))], axis=1)
    cdd = np.concatenate([cd, cd, cd, cd], axis=1)
    sdd = np.concatenate([-sd, sd, -sd, sd], axis=1)
    return jnp.asarray(np.concatenate([cq, sq, cdd, sdd], axis=1), F32)


def kernel(x, c, ctx, c_ctx, w_mod, b_mod, g_attn, g_ffn, w_in, g_q_lat, w_uq, g_kv_lat, w_ukv, lam_q1, lam_k1,
           lam_q2, lam_k2, g_subln, w_out, w_router, router_bias, w1, w3, w2, ws1, ws3, ws2, g_final):
    n_batch, n_lat, d = x.shape
    n_ctx = ctx.shape[1]
    t = n_batch * n_lat
    tm = 256
    tq = 512
    bm = 256
    td = 128
    assert w_mod.shape[0] == 1 and n_ctx % tm == 0 and n_lat % tq == 0 and n_lat % tm == 0 and t % td == 0
    assert (t * TOP_K) % bm == 0

    cc = jnp.concatenate([c, c_ctx[None, :], jnp.zeros((8 - n_batch - 1, d), F32)], axis=0)
    mod = _modulation(cc, w_mod[0], b_mod)

    w_in_ext, w_uq_ext, w_ukv_ext = _layout_weights(w_in[0], w_uq[0], w_ukv[0])
    tables = _rotary_tables(n_ctx, n_lat)
    qm, km, vmt, qd, k12, vdt = _projections(ctx, x, mod, g_attn, w_in_ext, g_q_lat, w_uq_ext, g_kv_lat,
                                             w_ukv_ext, tables, tm)
    om = _attn_mla(qm, km, vmt, tq)
    od = _attn_diff((lam_q1, lam_k1, lam_q2, lam_k2), g_subln.reshape(-1, 1), qd, k12, vdt, tq)

    ws13 = jnp.concatenate([ws1[0], ws3[0]], axis=1).astype(BF16)
    xp, h2p, eidx, rank, wts, cnt = _post(
        om.reshape(t, -1), od.reshape(t, -1), x.reshape(t, d), mod, w_out[0].astype(BF16), g_ffn, ws13,
        ws2[0].astype(BF16), w_router[0].T, router_bias[0][:, None], n_lat, tm)

    counts = cnt[:, 0]
    start = (jnp.cumsum(counts) - counts).astype(jnp.int32)
    n_slots = t * TOP_K
    n_blocks = n_slots // bm
    blk_b = jnp.arange(n_blocks, dtype=jnp.int32) * bm
    exp_b = jnp.concatenate([start[1:], jnp.full((1,), n_slots, jnp.int32)])
    pos_blk = jnp.arange(n_blocks, dtype=jnp.int32) + jnp.sum(exp_b[None, :] < blk_b[:, None], axis=1)
    pos_exp = jnp.arange(N_EXPERTS, dtype=jnp.int32) + jnp.sum(blk_b[None, :] <= exp_b[:, None], axis=1)
    slot = jnp.arange(n_blocks + N_EXPERTS, dtype=jnp.int32)[:, None]
    bounds = (jnp.sum(jnp.where(pos_blk[None, :] == slot, blk_b[None, :], 0), axis=1)
              + jnp.sum(jnp.where(pos_exp[None, :] == slot, exp_b[None, :], 0), axis=1))
    item_lo, item_hi = bounds[:-1], bounds[1:]
    item_blk = jnp.minimum(item_lo // bm, n_blocks - 1)
    item_e = jnp.clip(jnp.sum(start[None, :] <= item_lo[:, None], axis=1) - 1, 0, N_EXPERTS - 1).astype(jnp.int32)
    changed = jnp.concatenate([jnp.zeros((1,), jnp.int32), (item_e[1:] != item_e[:-1]).astype(jnp.int32)])
    item_slot = jnp.cumsum(changed).astype(jnp.int32) % 2
    later = jnp.where(item_e[None, :] > item_e[:, None], item_e[None, :], N_EXPERTS)
    item_next = jnp.min(later, axis=1)
    item_next = jnp.where(item_next == N_EXPERTS, -1, item_next).astype(jnp.int32)

    dest = _slots(start, eidx, rank)
    xs = _dispatch(h2p, dest, td)
    blk_ids = jnp.arange(n_blocks + 1, dtype=jnp.int32)
    item_first = jnp.sum(item_blk[None, :] < blk_ids[:, None], axis=1).astype(jnp.int32)
    ys = _experts((item_first, item_e, item_lo, item_hi, item_slot, item_next), xs, w1[0], w3[0], w2[0], bm)
    out = _combine(xp, wts.T, mod, g_final[None, :], dest, ys, n_lat, td)
    return out.reshape(n_batch, n_lat, d)
```

```python
import functools
import math

import jax
import jax.numpy as jnp
import numpy as np
from jax import lax
from jax.experimental import pallas as pl
from jax.experimental.pallas import tpu as pltpu

GRID_W = 64
ROPE_BASE = 10000.0
NORM_EPS = 1e-6
MLA_HEADS = 8
MLA_NOPE = 64
MLA_ROPE = 32
MLA_V = 64
MLA_Q_RANK = 256
MLA_KV_RANK = 128
DIFF_HEADS = 4
DIFF_QK = 64
DIFF_V = 2 * DIFF_QK
N_EXPERTS = 256
TOP_K = 8
N_GROUPS = 8
TOPK_GROUPS = 4
ROUTED_SCALE = 2.5
LAM_INIT = 0.8 - 0.6 * math.exp(-0.3 * 0)

LANES = 128
VMEM_LIMIT = 48 * 1024 * 1024

F32 = jnp.float32
BF16 = jnp.bfloat16
NEG_INF = float("-inf")


def _cparams(sem):
    return pltpu.CompilerParams(dimension_semantics=sem, vmem_limit_bytes=VMEM_LIMIT)


def _rms_rows(x, g):
    return x * lax.rsqrt(jnp.mean(x * x, axis=-1, keepdims=True) + NORM_EPS) * g


def _dot(a, b):
    return jnp.dot(a, b, preferred_element_type=F32)


def _dot_nt(a, b):
    return lax.dot_general(a, b, (((1,), (1,)), ((), ())), preferred_element_type=F32)


def _mod_kernel(c_ref, w_ref, b_ref, o_ref):
    a = c_ref[...]
    a = a * jax.nn.sigmoid(a)
    o_ref[...] = jnp.dot(a, w_ref[...], preferred_element_type=F32,
                         precision=lax.Precision.HIGHEST) + b_ref[...]


def _modulation(cc, w_mod, b_mod):
    rows, d = cc.shape
    cols = w_mod.shape[1]
    tn = 1536
    return pl.pallas_call(
        _mod_kernel,
        out_shape=jax.ShapeDtypeStruct((rows, cols), F32),
        grid=(cols // tn,),
        in_specs=[pl.BlockSpec((rows, d), lambda j: (0, 0)),
                  pl.BlockSpec((d, tn), lambda j: (0, j)),
                  pl.BlockSpec((1, tn), lambda j: (0, j))],
        out_specs=pl.BlockSpec((rows, tn), lambda j: (0, j)),
        compiler_params=_cparams(("arbitrary",)),
        name="mod",
    )(cc, w_mod, b_mod)


_C_Q = 0
_C_KV = _C_Q + MLA_Q_RANK
_C_DQ = _C_KV + MLA_KV_RANK
_C_DQS = _C_DQ + DIFF_HEADS * 2 * DIFF_QK
_C_DK = _C_DQS + DIFF_HEADS * 2 * DIFF_QK
_C_DKS = _C_DK + DIFF_HEADS * 2 * DIFF_QK
_C_DV = _C_DKS + DIFF_HEADS * 2 * DIFF_QK
_C_KRA = _C_DV + DIFF_HEADS * DIFF_V
_C_KRB = _C_KRA + LANES
_IN_EXT = _C_KRB + LANES
_N_TAB = 4


def _proj_kernel(n_batch, d_model, ctx_ref, x_ref, mod_ref, g_attn_ref, w_in_ref, g_q_ref, w_uq_ref,
                 g_kv_ref, w_ukv_ref, tab_ref,
                 qm_ref, km_ref, vmt_ref, qd_ref, k12_ref, vdt_ref, *, n_ctx_tiles):
    b = pl.program_id(0)
    i = pl.program_id(1)
    is_ctx = i < n_ctx_tiles
    xin = jnp.where(is_ctx, ctx_ref[0], x_ref[0])
    row = jnp.where(is_ctx, n_batch, b)
    sh1 = mod_ref[pl.ds(row, 1), 0:d_model]
    sc1 = mod_ref[pl.ds(row, 1), d_model:2 * d_model]
    h = _rms_rows(xin, g_attn_ref[...]) * (1.0 + sc1) + sh1
    p = _dot(h.astype(BF16), w_in_ref[...])

    cq = _rms_rows(p[:, _C_Q:_C_Q + MLA_Q_RANK], g_q_ref[...])
    qa = _dot(cq.astype(BF16), w_uq_ref[...])
    ckv = _rms_rows(p[:, _C_KV:_C_KV + MLA_KV_RANK], g_kv_ref[...])
    kv = _dot(ckv.astype(BF16), w_ukv_ref[...])

    def tab(j):
        return tab_ref[:, j * LANES:(j + 1) * LANES]

    kr = p[:, _C_KRA:_C_KRA + LANES] * tab(0) + p[:, _C_KRB:_C_KRB + LANES] * tab(1)
    first_half = lax.broadcasted_iota(jnp.int32, (1, LANES), 1) < DIFF_QK
    hw = MLA_HEADS * LANES
    for hd in range(MLA_HEADS):
        lo = hd * LANES
        qm_ref[0, hd] = (qa[:, lo:lo + LANES] * tab(0) + qa[:, hw + lo:hw + lo + LANES] * tab(1)).astype(BF16)
        km_ref[0, hd] = (kv[:, lo:lo + LANES] + kr).astype(BF16)
    for pr in range(MLA_HEADS // 2):
        lo = hw + pr * LANES
        vmt_ref[0, pr] = kv[:, lo:lo + LANES].T.astype(BF16)
    for hd in range(DIFF_HEADS):
        lo = hd * LANES
        dq = p[:, _C_DQ + lo:_C_DQ + lo + LANES]
        dqs = p[:, _C_DQS + lo:_C_DQS + lo + LANES]
        qr = dq * tab(2) + dqs * tab(3)
        qd_ref[0, hd, 0] = jnp.where(first_half, qr, 0.0).astype(BF16)
        qd_ref[0, hd, 1] = jnp.where(first_half, 0.0, qr).astype(BF16)
        dk = p[:, _C_DK + lo:_C_DK + lo + LANES]
        dks = p[:, _C_DKS + lo:_C_DKS + lo + LANES]
        k12_ref[0, hd] = (dk * tab(2) + dks * tab(3)).astype(BF16)
        vdt_ref[0, hd] = p[:, _C_DV + lo:_C_DV + lo + LANES].T.astype(BF16)


def _projections(ctx, x, mod, g_attn, w_in_ext, g_q, w_uq_ext, g_kv, w_ukv_ext, tables, tm):
    n_batch, n_ctx, d = ctx.shape
    n_lat = x.shape[1]
    n_all = n_ctx + n_lat
    nct = n_ctx // tm
    nt = n_all // tm

    def full(a):
        return pl.BlockSpec(a.shape, lambda b, i: (0,) * a.ndim)

    def k_spec(heads):
        return pl.BlockSpec((1, heads, tm, LANES), lambda b, i: (b, 0, i, 0))

    def vt_spec(heads):
        return pl.BlockSpec((1, heads, LANES, tm), lambda b, i: (b, 0, 0, i))

    def lat(i):
        return jnp.maximum(i - nct, 0)

    def sds(*shape):
        return jax.ShapeDtypeStruct((n_batch,) + shape, BF16)

    return pl.pallas_call(
        functools.partial(_proj_kernel, n_batch, d, n_ctx_tiles=nct),
        out_shape=(sds(MLA_HEADS, n_lat, LANES), sds(MLA_HEADS, n_all, LANES), sds(MLA_HEADS // 2, LANES, n_all),
                   sds(DIFF_HEADS, 2, n_lat, LANES), sds(DIFF_HEADS, n_all, LANES),
                   sds(DIFF_HEADS, LANES, n_all)),
        grid=(n_batch, nt),
        in_specs=[pl.BlockSpec((1, tm, d), lambda b, i: (b, jnp.minimum(i, nct - 1), 0)),
                  pl.BlockSpec((1, tm, d), lambda b, i: (b, jnp.maximum(i - nct, 0), 0)),
                  full(mod), full(g_attn), full(w_in_ext), full(g_q), full(w_uq_ext), full(g_kv),
                  full(w_ukv_ext),
                  pl.BlockSpec((tm, _N_TAB * LANES), lambda b, i: (i, 0))],
        out_specs=(pl.BlockSpec((1, MLA_HEADS, tm, LANES), lambda b, i: (b, 0, lat(i), 0)),
                   k_spec(MLA_HEADS), vt_spec(MLA_HEADS // 2),
                   pl.BlockSpec((1, DIFF_HEADS, 2, tm, LANES), lambda b, i: (b, 0, 0, lat(i), 0)),
                   k_spec(DIFF_HEADS), vt_spec(DIFF_HEADS)),
        compiler_params=_cparams(("arbitrary", "arbitrary")),
        name="proj",
    )(ctx, x, mod, g_attn, w_in_ext, g_q, w_uq_ext, g_kv, w_ukv_ext, tables)


def _attend_t(q, k, vt):
    st = _dot_nt(k, q)
    et = jnp.exp2(st - jnp.max(st, axis=0, keepdims=True))
    l = jnp.sum(et, axis=0, keepdims=True)
    return _dot(vt, et.astype(BF16)) * (1.0 / l)


def _attn_mla_kernel(q_ref, k_ref, vt_ref, o_ref):
    sts = [_dot_nt(k_ref[0, j], q_ref[0, j]) for j in range(2)]
    ets = [jnp.exp2(st - jnp.max(st, axis=0, keepdims=True)) for st in sts]
    ls = [jnp.sum(et, axis=0, keepdims=True) for et in ets]
    outs = [_dot(vt_ref[0, 0, j * MLA_V:(j + 1) * MLA_V, :], ets[j].astype(BF16)) * (1.0 / ls[j]) for j in range(2)]
    o_ref[0] = jnp.concatenate(outs, axis=0).T.astype(BF16)


def _attn_mla(qm, km, vmt, tq):
    n_batch, heads, n_lat, _ = qm.shape
    n_all = km.shape[2]
    return pl.pallas_call(
        _attn_mla_kernel,
        out_shape=jax.ShapeDtypeStruct((n_batch, n_lat, heads // 2 * LANES), BF16),
        grid=(n_batch, heads // 2, n_lat // tq),
        in_specs=[pl.BlockSpec((1, 2, tq, LANES), lambda b, h, i: (b, h, i, 0)),
                  pl.BlockSpec((1, 2, n_all, LANES), lambda b, h, i: (b, h, 0, 0)),
                  pl.BlockSpec((1, 1, LANES, n_all), lambda b, h, i: (b, h, 0, 0))],
        out_specs=pl.BlockSpec((1, tq, LANES), lambda b, h, i: (b, i, h)),
        compiler_params=_cparams(("arbitrary", "arbitrary", "arbitrary")),
        name="attn_m",
    )(qm, km, vmt)


def _attn_diff_kernel(lq1_ref, lk1_ref, lq2_ref, lk2_ref, g_sub_ref, q_ref, k_ref, vt_ref, o_ref):
    lam = (jnp.exp(jnp.sum(lq1_ref[...] * lk1_ref[...], axis=-1, keepdims=True))
           - jnp.exp(jnp.sum(lq2_ref[...] * lk2_ref[...], axis=-1, keepdims=True)) + LAM_INIT)
    tq = q_ref.shape[3]
    o = _attend_t(q_ref[0, 0].reshape(2 * tq, LANES), k_ref[0, 0], vt_ref[0, 0])
    ot = o[:, :tq] - lam * o[:, tq:]
    ot = ot * lax.rsqrt(jnp.mean(ot * ot, axis=0, keepdims=True) + NORM_EPS) * g_sub_ref[...]
    o_ref[0] = (ot * (1.0 - LAM_INIT)).T.astype(BF16)


def _attn_diff(lams, g_sub_col, qd, k12, vdt, tq):
    n_batch, heads, _, n_lat, _ = qd.shape
    n_all = k12.shape[2]

    def small(a):
        return pl.BlockSpec(a.shape, lambda b, h, i: (0,) * a.ndim)

    return pl.pallas_call(
        _attn_diff_kernel,
        out_shape=jax.ShapeDtypeStruct((n_batch, n_lat, heads * LANES), BF16),
        grid=(n_batch, heads, n_lat // tq),
        in_specs=[small(lams[0]), small(lams[1]), small(lams[2]), small(lams[3]), small(g_sub_col),
                  pl.BlockSpec((1, 1, 2, tq, LANES), lambda b, h, i: (b, h, 0, i, 0)),
                  pl.BlockSpec((1, 1, n_all, LANES), lambda b, h, i: (b, h, 0, 0)),
                  pl.BlockSpec((1, 1, LANES, n_all), lambda b, h, i: (b, h, 0, 0))],
        out_specs=pl.BlockSpec((1, tq, LANES), lambda b, h, i: (b, i, h)),
        compiler_params=_cparams(("arbitrary", "arbitrary", "arbitrary")),
        name="attn_d",
    )(*lams, g_sub_col, qd, k12, vdt)


def _post_kernel(d_model, tiles_per_batch, om_ref, od_ref, x_ref, mod_ref, w_out_ref, g_ffn_ref, ws13_ref,
                 ws2_ref, wr_hi_ref, wr_lo_ref, br_ref,
                 xp_ref, h2p_ref, eidx_ref, rank_ref, wts_ref, cnt_ref, carry_ref):
    i = pl.program_id(0)
    tm = x_ref.shape[0]
    n_exp = wr_hi_ref.shape[0]
    per_group = n_exp // N_GROUPS

    @pl.when(i == 0)
    def _():
        carry_ref[...] = jnp.zeros_like(carry_ref)

    b = i // tiles_per_batch

    def modv(j):
        return mod_ref[pl.ds(b, 1), j * d_model:(j + 1) * d_model]

    half = om_ref.shape[1]
    y = _dot(om_ref[...], w_out_ref[0:half, :]) + _dot(od_ref[...], w_out_ref[half:, :])
    x1 = x_ref[...] + modv(2) * y
    h2 = _rms_rows(x1, g_ffn_ref[...]) * (1.0 + modv(4)) + modv(3)
    h2p_ref[...] = _pack_pair(h2[:, :d_model // 2], h2[:, d_model // 2:])

    h2b = h2.astype(BF16)
    gu = _dot(h2b, ws13_ref[...])
    ff = gu.shape[1] // 2
    g = gu[:, :ff]
    act = g * jax.nn.sigmoid(g) * gu[:, ff:]
    shared = _dot(act.astype(BF16), ws2_ref[...])
    xp_ref[...] = x1 + modv(5) * shared

    h2_top = lax.bitcast_convert_type(lax.bitcast_convert_type(h2, jnp.uint32) & jnp.uint32(_HI_MASK), F32)
    h2_hi = h2_top.astype(BF16)
    h2_lo = (h2 - h2_top).astype(BF16)
    logits = _dot_nt(wr_hi_ref[...], h2_hi) + (_dot_nt(wr_hi_ref[...], h2_lo)
                                               + _dot_nt(wr_lo_ref[...], h2_hi))
    s = jax.nn.sigmoid(logits)
    ssel = s + br_ref[...]
    s3 = ssel.reshape(N_GROUPS, per_group, tm)
    m1 = jnp.max(s3, axis=1, keepdims=True)
    eq = s3 == m1
    n_eq = jnp.sum(jnp.where(eq, 1.0, 0.0), axis=1, keepdims=True)
    m2 = jnp.max(jnp.where(eq, NEG_INF, s3), axis=1, keepdims=True)
    grp = m1 + jnp.where(n_eq >= 2.0, m1, m2)

    gi = lax.broadcasted_iota(jnp.int32, grp.shape, 0).astype(F32)
    gcur = grp
    gsel = jnp.zeros_like(grp)
    for _ in range(TOPK_GROUPS):
        gm = jnp.max(gcur, axis=0, keepdims=True)
        first = jnp.min(jnp.where(gcur == gm, gi, float(N_GROUPS)), axis=0, keepdims=True)
        oh = gi == first
        gsel = jnp.where(oh, 1.0, gsel)
        gcur = jnp.where(oh, NEG_INF, gcur)
    cur = jnp.where(gsel > 0.0, s3, NEG_INF).reshape(n_exp, tm)

    ie = lax.broadcasted_iota(jnp.int32, (n_exp, tm), 0).astype(F32)
    sel = jnp.zeros((n_exp, tm), F32)
    e_rows = []
    s_rows = []
    for _ in range(TOP_K):
        m = jnp.max(cur, axis=0, keepdims=True)
        first = jnp.min(jnp.where(cur == m, ie, float(n_exp)), axis=0, keepdims=True)
        oh = ie == first
        cur = jnp.where(oh, NEG_INF, cur)
        sel = jnp.where(oh, 1.0, sel)
        e_rows.append(first)
        s_rows.append(jnp.sum(jnp.where(oh, s, 0.0), axis=0, keepdims=True))
    s_tot = s_rows[0]
    for r in s_rows[1:]:
        s_tot = s_tot + r

    tr = lax.broadcasted_iota(jnp.int32, (tm, tm), 0)
    tc = lax.broadcasted_iota(jnp.int32, (tm, tm), 1)
    upper = jnp.where(tr < tc, 1.0, 0.0).astype(BF16)
    rank = _dot(sel.astype(BF16), upper) + carry_ref[...]
    for k in range(TOP_K):
        eidx_ref[k:k + 1, :] = e_rows[k].astype(jnp.int32)
        rank_ref[k:k + 1, :] = jnp.sum(jnp.where(ie == e_rows[k], rank, 0.0), axis=0,
                                       keepdims=True).astype(jnp.int32)
        wts_ref[k:k + 1, :] = s_rows[k] / s_tot * ROUTED_SCALE
    carry_ref[...] = carry_ref[...] + jnp.sum(sel, axis=1, keepdims=True)
    cnt_ref[...] = carry_ref[...].astype(jnp.int32)


def _post(om, od, x2d, mod, w_out, g_ffn, ws13, ws2, wr_t, br, n_lat, tm):
    t, d = x2d.shape
    n_exp = wr_t.shape[0]
    wr_top = lax.bitcast_convert_type(lax.bitcast_convert_type(wr_t, jnp.uint32) & jnp.uint32(_HI_MASK), F32)
    wr_hi = wr_top.astype(BF16)
    wr_lo = (wr_t - wr_top).astype(BF16)

    def full(a):
        return pl.BlockSpec(a.shape, lambda i: (0,) * a.ndim)

    row = lambda w: pl.BlockSpec((tm, w), lambda i: (i, 0))
    col = pl.BlockSpec((TOP_K, tm), lambda i: (0, i))
    return pl.pallas_call(
        functools.partial(_post_kernel, d, n_lat // tm),
        out_shape=(jax.ShapeDtypeStruct((t, d), F32), jax.ShapeDtypeStruct((t, d // 2), jnp.uint32),
                   jax.ShapeDtypeStruct((TOP_K, t), jnp.int32), jax.ShapeDtypeStruct((TOP_K, t), jnp.int32),
                   jax.ShapeDtypeStruct((TOP_K, t), F32), jax.ShapeDtypeStruct((n_exp, 1), jnp.int32)),
        grid=(t // tm,),
        in_specs=[row(om.shape[1]), row(od.shape[1]), row(d), full(mod), full(w_out), full(g_ffn),
                  full(ws13), full(ws2), full(wr_hi), full(wr_lo), full(br)],
        out_specs=(row(d), row(d // 2), col, col, col, pl.BlockSpec((n_exp, 1), lambda i: (0, 0))),
        scratch_shapes=[pltpu.VMEM((n_exp, 1), F32)],
        compiler_params=_cparams(("arbitrary",)),
        name="post",
    )(om, od, x2d, mod, w_out, g_ffn, ws13, ws2, wr_hi, wr_lo, br)


_HI_MASK = 0xFFFF0000


def _pack_pair(a, b):
    lo = lax.bitcast_convert_type(a.astype(BF16).astype(F32), jnp.uint32) >> 16
    hi = lax.bitcast_convert_type(b.astype(BF16).astype(F32), jnp.uint32) & jnp.uint32(_HI_MASK)
    return lo | hi


def _unpack_pair(w):
    return (lax.bitcast_convert_type(w << 16, F32),
            lax.bitcast_convert_type(w & jnp.uint32(_HI_MASK), F32))


def _row_copy(src, src_row, dst, dst_row, sem):
    return pltpu.make_async_copy(src.at[pl.ds(src_row, 1), :], dst.at[pl.ds(dst_row, 1), :], sem)


def _slots_kernel(start_ref, eidx_ref, rank_ref, dest_ref):
    e = eidx_ref[...]

    def body(x, acc):
        return jnp.where(e == x, start_ref[x], acc)

    dest_ref[...] = lax.fori_loop(0, start_ref.shape[0], body, jnp.zeros_like(e), unroll=8) + rank_ref[...]


def _slots(start, eidx, rank):
    k, t = eidx.shape
    tt = min(t, 2048)
    spec = pl.BlockSpec((k, tt), lambda i, *_: (0, i))
    return pl.pallas_call(
        _slots_kernel,
        out_shape=jax.ShapeDtypeStruct((k, t), jnp.int32),
        grid_spec=pltpu.PrefetchScalarGridSpec(num_scalar_prefetch=1, grid=(t // tt,), in_specs=[spec, spec],
                                               out_specs=spec),
        compiler_params=_cparams(("arbitrary",)),
        name="slots",
    )(start, eidx, rank)


def _dest_copy(t, td, dest_hbm, d_s, sem):
    return pltpu.make_async_copy(dest_hbm.at[:, pl.ds(t * td, td)], d_s.at[t % 2], sem.at[t % 2])


def _dispatch_kernel(h2p_hbm, dest_hbm, xs_hbm, hbuf, d_s, fsem, rsem, dsem):
    i = pl.program_id(0)
    n = pl.num_programs(0)
    td = hbuf.shape[1]

    def fetch(t):
        slot = t % 3
        return pltpu.make_async_copy(h2p_hbm.at[pl.ds(t * td, td), :], hbuf.at[slot], fsem.at[slot])

    def wait_rows(t):
        for _ in range(td * TOP_K):
            _row_copy(hbuf.at[0], 0, xs_hbm, 0, dsem.at[t % 2]).wait()

    @pl.when(i == 0)
    def _():
        fetch(0).start()
        _dest_copy(0, td, dest_hbm, d_s, rsem).start()

        @pl.when(n > 1)
        def _():
            fetch(1).start()

    @pl.when(i + 1 < n)
    def _():
        _dest_copy(i + 1, td, dest_hbm, d_s, rsem).start()

    fetch(i).wait()
    _dest_copy(i, td, dest_hbm, d_s, rsem).wait()

    src = hbuf.at[i % 3]
    dst = d_s.at[i % 2]
    sem = dsem.at[i % 2]
    for j in range(td):
        for k in range(TOP_K):
            _row_copy(src, j, xs_hbm, dst[k, j], sem).start(priority=(j * TOP_K + k) % 2)

    @pl.when(i >= 1)
    def _():
        wait_rows(i - 1)

    @pl.when(i + 2 < n)
    def _():
        fetch(i + 2).start()

    @pl.when(i == n - 1)
    def _():
        wait_rows(i)


def _dispatch(h2p, dest, td):
    t, dw = h2p.shape
    any_spec = pl.BlockSpec(memory_space=pl.ANY)
    return pl.pallas_call(
        _dispatch_kernel,
        out_shape=jax.ShapeDtypeStruct((t * TOP_K, dw), jnp.uint32),
        grid=(t // td,),
        in_specs=[any_spec, any_spec],
        out_specs=any_spec,
        scratch_shapes=[pltpu.VMEM((3, td, dw), jnp.uint32), pltpu.SMEM((2, TOP_K, td), jnp.int32),
                        pltpu.SemaphoreType.DMA((3,)), pltpu.SemaphoreType.DMA((2,)),
                        pltpu.SemaphoreType.DMA((2,))],
        compiler_params=_cparams(("arbitrary",)),
        name="dispatch",
    )(h2p, dest)


def _experts_kernel(first_ref, e_ref, lo_ref, hi_ref, slot_ref, next_ref, next2_ref, xs_ref, w1_hbm, w3_hbm, w2_hbm, ys_ref,
                    w1_f, w3_f, w2_f, w13_b, w2_b, wsem):
    b = pl.program_id(0)
    bm, dw = xs_ref.shape
    ff = w1_hbm.shape[2]
    base = b * bm

    def fetch(e, slot):
        return (pltpu.make_async_copy(w1_hbm.at[e], w1_f.at[slot], wsem.at[slot, 0]),
                pltpu.make_async_copy(w3_hbm.at[e], w3_f.at[slot], wsem.at[slot, 1]),
                pltpu.make_async_copy(w2_hbm.at[e], w2_f.at[slot], wsem.at[slot, 2]))

    def swiglu():
        x_lo, x_hi = _unpack_pair(xs_ref[...])
        gu = _dot(x_lo.astype(BF16), w13_b[:dw, :]) + _dot(x_hi.astype(BF16), w13_b[dw:, :])
        g = gu[:, :ff]
        act = g * jax.nn.sigmoid(g) * gu[:, ff:]
        y = _dot(act.astype(BF16), w2_b[...])
        return _pack_pair(y[:, :dw], y[:, dw:])

    def item(j, carry):
        lo = lo_ref[j]
        hi = hi_ref[j]

        @pl.when(j == 0)
        def _():
            for cp in fetch(e_ref[0], 0):
                cp.start()

            @pl.when(next_ref[0] >= 0)
            def _():
                for cp in fetch(next_ref[0], 1):
                    cp.start(priority=1)

        @pl.when(jnp.logical_or(j == 0, e_ref[j] != e_ref[jnp.maximum(j - 1, 0)]))
        def _():
            slot = slot_ref[j]
            for cp in fetch(e_ref[j], slot):
                cp.wait()

            @pl.when(next2_ref[j] >= 0)
            def _():
                for cp in fetch(next2_ref[j], (slot + 2) % 3):
                    cp.start(priority=1)

            w13_b[:, :ff] = w1_f[slot].astype(BF16)
            w13_b[:, ff:] = w3_f[slot].astype(BF16)
            w2_b[...] = w2_f[slot].astype(BF16)

        @pl.when(jnp.logical_and(hi > lo, lo == base))
        def _():
            ys_ref[...] = swiglu()

        @pl.when(jnp.logical_and(hi > lo, lo != base))
        def _():
            rows = base + lax.broadcasted_iota(jnp.int32, (bm, 1), 0)
            ys_ref[...] = jnp.where(rows >= lo, swiglu(), ys_ref[...])

        return carry

    lax.fori_loop(first_ref[b], first_ref[b + 1], item, 0)


def _experts(items, xs, w1, w3, w2, bm):
    n_slots, dw = xs.shape
    _, d, ff = w1.shape
    any_spec = pl.BlockSpec(memory_space=pl.ANY)
    rows = pl.BlockSpec((bm, dw), lambda b, *_: (b, 0))
    return pl.pallas_call(
        _experts_kernel,
        out_shape=jax.ShapeDtypeStruct((n_slots, dw), jnp.uint32),
        grid_spec=pltpu.PrefetchScalarGridSpec(
            num_scalar_prefetch=len(items),
            grid=(n_slots // bm,),
            in_specs=[rows, any_spec, any_spec, any_spec],
            out_specs=rows,
            scratch_shapes=[pltpu.VMEM((3, d, ff), F32), pltpu.VMEM((3, d, ff), F32), pltpu.VMEM((3, ff, d), F32),
                            pltpu.VMEM((d, 2 * ff), BF16), pltpu.VMEM((ff, d), BF16),
                            pltpu.SemaphoreType.DMA((3, 3))]),
        compiler_params=_cparams(("arbitrary",)),
        name="experts",
    )(*items, xs, w1, w3, w2)


def _combine_kernel(d_model, tiles_per_batch, n_tiles, xp_ref, wts_ref, mod_ref, g_final_ref, dest_hbm, ys_hbm,
                    o_ref, gbuf0, gbuf1, d_s, rsem, gsem):
    s = pl.program_id(0)
    tc = xp_ref.shape[0]
    dw = d_model // 2
    gbufs = (gbuf0, gbuf1)
    n_parts = TOP_K

    @pl.when(s == 0)
    def _():
        _dest_copy(0, tc, dest_hbm, d_s, rsem).start()

    @pl.when(s + 1 < n_tiles)
    def _():
        _dest_copy(s + 1, tc, dest_hbm, d_s, rsem).start()

    def step(par, do_issue, do_reduce):
        cur, prev = gbufs[par], gbufs[1 - par]
        if do_issue:
            _dest_copy(s, tc, dest_hbm, d_s, rsem).wait()
        if do_reduce:
            for _ in range(tc * TOP_K):
                _row_copy(ys_hbm, 0, prev.at[0], 0, gsem.at[1 - par]).wait()
            w = wts_ref[...]
        r_lo = r_hi = None
        for part in range(n_parts):
            if do_issue:
                for j in range(part * tc // n_parts, (part + 1) * tc // n_parts):
                    for k in range(TOP_K):
                        _row_copy(ys_hbm, d_s[par, k, j], cur.at[k], j,
                                  gsem.at[par]).start(priority=(j * TOP_K + k) % 2)
            if do_reduce:
                y_lo, y_hi = _unpack_pair(prev[part])
                wk = w[:, part:part + 1]
                r_lo = wk * y_lo if r_lo is None else r_lo + wk * y_lo
                r_hi = wk * y_hi if r_hi is None else r_hi + wk * y_hi
        if do_reduce:
            b = (s - 1) // tiles_per_batch
            gt2 = mod_ref[pl.ds(b, 1), 5 * d_model:6 * d_model]
            v_lo = xp_ref[:, :dw] + gt2[:, :dw] * r_lo
            v_hi = xp_ref[:, dw:] + gt2[:, dw:] * r_hi
            ms = (jnp.sum(v_lo * v_lo, axis=-1, keepdims=True)
                  + jnp.sum(v_hi * v_hi, axis=-1, keepdims=True)) / d_model
            inv = lax.rsqrt(ms + NORM_EPS)
            o_ref[:, :dw] = v_lo * inv * g_final_ref[:, :dw]
            o_ref[:, dw:] = v_hi * inv * g_final_ref[:, dw:]

    @pl.when(s == 0)
    def _():
        step(0, True, False)

    @pl.when(jnp.logical_and(jnp.logical_and(s >= 1, s < n_tiles), s % 2 == 0))
    def _():
        step(0, True, True)

    @pl.when(jnp.logical_and(s < n_tiles, s % 2 == 1))
    def _():
        step(1, True, True)

    @pl.when(s == n_tiles)
    def _():
        step(n_tiles % 2, False, True)


def _combine(xp, wts_t, mod, g_final, dest, ys, n_lat, tc):
    t, d = xp.shape
    any_spec = pl.BlockSpec(memory_space=pl.ANY)
    prev = lambda s: (jnp.maximum(s - 1, 0), 0)
    gbuf = pltpu.VMEM((TOP_K, tc, d // 2), jnp.uint32)
    return pl.pallas_call(
        functools.partial(_combine_kernel, d, n_lat // tc, t // tc),
        out_shape=jax.ShapeDtypeStruct((t, d), F32),
        grid=(t // tc + 1,),
        in_specs=[pl.BlockSpec((tc, d), prev), pl.BlockSpec((tc, TOP_K), prev),
                  pl.BlockSpec(mod.shape, lambda s: (0, 0)), pl.BlockSpec(g_final.shape, lambda s: (0, 0)),
                  any_spec, any_spec],
        out_specs=pl.BlockSpec((tc, d), prev),
        scratch_shapes=[gbuf, gbuf, pltpu.SMEM((2, TOP_K, tc), jnp.int32),
                        pltpu.SemaphoreType.DMA((2,)), pltpu.SemaphoreType.DMA((2,))],
        compiler_params=_cparams(("arbitrary",)),
        name="combine",
    )(xp, wts_t, mod, g_final, dest, ys)


def _swap_halves(w, width):
    r, c = w.shape
    return w.reshape(r, c // width, 2, width // 2)[:, :, ::-1, :].reshape(r, c)


def _pad_cols(w, left, total):
    return jnp.pad(w, ((0, 0), (left, total - left - w.shape[1])))


def _layout_weights(w_in, w_uq, w_ukv):
    s0 = MLA_Q_RANK
    s1 = s0 + MLA_KV_RANK
    s2 = s1 + MLA_ROPE
    s3 = s2 + DIFF_HEADS * 2 * DIFF_QK
    s4 = s3 + DIFF_HEADS * 2 * DIFF_QK
    log2e = 1.0 / math.log(2.0)
    w_kr, w_dq, w_dk = w_in[:, s1:s2], w_in[:, s2:s3] * (log2e / math.sqrt(DIFF_QK)), w_in[:, s3:s4]
    w_uq = w_uq * (log2e / math.sqrt(MLA_NOPE + MLA_ROPE))
    w_in_ext = jnp.concatenate([
        w_in[:, :s1], w_dq, _swap_halves(w_dq, DIFF_QK), w_dk, _swap_halves(w_dk, DIFF_QK), w_in[:, s4:],
        _pad_cols(w_kr, MLA_NOPE, LANES), _pad_cols(_swap_halves(w_kr, MLA_ROPE), MLA_NOPE, LANES)],
        axis=1).astype(BF16)

    qk = MLA_NOPE + MLA_ROPE
    qa, qb, kk, vv = [], [], [], []
    for h in range(MLA_HEADS):
        wq = w_uq[:, h * qk:(h + 1) * qk]
        qa.append(_pad_cols(wq, 0, LANES))
        qb.append(_pad_cols(_swap_halves(wq[:, MLA_NOPE:], MLA_ROPE), MLA_NOPE, LANES))
        wkv = w_ukv[:, h * (MLA_NOPE + MLA_V):(h + 1) * (MLA_NOPE + MLA_V)]
        kk.append(_pad_cols(wkv[:, :MLA_NOPE], 0, LANES))
        vv.append(wkv[:, MLA_NOPE:])
    w_uq_ext = jnp.concatenate(qa + qb, axis=1).astype(BF16)
    w_ukv_ext = jnp.concatenate(kk + vv, axis=1).astype(BF16)
    return w_in_ext, w_uq_ext, w_ukv_ext


def _rotary_tables(n_ctx, n_lat):
    def angles(rot_dim):
        n_freq = rot_dim // 4
        inv = ROPE_BASE ** (-(np.arange(n_freq, dtype=np.float64) / n_freq))
        rows = n_lat // GRID_W
        row = np.repeat(np.arange(rows, dtype=np.float64), GRID_W)
        col = np.tile(np.arange(GRID_W, dtype=np.float64), rows)
        theta = np.concatenate([row[:, None] * inv, col[:, None] * inv], axis=-1)
        theta = np.concatenate([np.zeros((n_ctx, 2 * n_freq)), theta], axis=0)
        return np.cos(theta), np.sin(theta)

    n = n_ctx + n_lat
    cm, sm = angles(MLA_ROPE)
    cd, sd = angles(DIFF_QK)
    pad_m = LANES - MLA_NOPE - MLA_ROPE
    cq = np.concatenate([np.ones((n, MLA_NOPE)), cm, cm, np.zeros((n, pad_m))], axis=1)
    sq = np.concatenate([np.zeros((n, MLA_NOPE)), -sm, sm, np.zeros((n, pad_m))], axis=1)
    cdd = np.concatenate([cd, cd, cd, cd], axis=1)
    sdd = np.concatenate([-sd, sd, -sd, sd], axis=1)
    return jnp.asarray(np.concatenate([cq, sq, cdd, sdd], axis=1), F32)


def kernel(x, c, ctx, c_ctx, w_mod, b_mod, g_attn, g_ffn, w_in, g_q_lat, w_uq, g_kv_lat, w_ukv, lam_q1, lam_k1,
           lam_q2, lam_k2, g_subln, w_out, w_router, router_bias, w1, w3, w2, ws1, ws3, ws2, g_final):
    n_batch, n_lat, d = x.shape
    n_ctx = ctx.shape[1]
    t = n_batch * n_lat
    tm = 256
    tp = 512
    tq = 512
    bm = 256
    td = 128
    assert w_mod.shape[0] == 1 and n_ctx % tm == 0 and n_lat % tq == 0 and n_lat % tm == 0 and t % td == 0
    assert (t * TOP_K) % bm == 0 and n_lat % tp == 0

    cc = jnp.concatenate([c, c_ctx[None, :], jnp.zeros((8 - n_batch - 1, d), F32)], axis=0)
    mod = _modulation(cc, w_mod[0], b_mod)

    w_in_ext, w_uq_ext, w_ukv_ext = _layout_weights(w_in[0], w_uq[0], w_ukv[0])
    tables = _rotary_tables(n_ctx, n_lat)
    qm, km, vmt, qd, k12, vdt = _projections(ctx, x, mod, g_attn, w_in_ext, g_q_lat, w_uq_ext, g_kv_lat,
                                             w_ukv_ext, tables, tm)
    om = _attn_mla(qm, km, vmt, tq)
    od = _attn_diff((lam_q1, lam_k1, lam_q2, lam_k2), g_subln.reshape(-1, 1), qd, k12, vdt, tq)

    ws13 = jnp.concatenate([ws1[0], ws3[0]], axis=1).astype(BF16)
    xp, h2p, eidx, rank, wts, cnt = _post(
        om.reshape(t, -1), od.reshape(t, -1), x.reshape(t, d), mod, w_out[0].astype(BF16), g_ffn, ws13,
        ws2[0].astype(BF16), w_router[0].T, router_bias[0][:, None], n_lat, tp)

    counts = cnt[:, 0]
    start = (jnp.cumsum(counts) - counts).astype(jnp.int32)
    n_slots = t * TOP_K
    n_blocks = n_slots // bm
    blk_b = jnp.arange(n_blocks, dtype=jnp.int32) * bm
    exp_b = jnp.concatenate([start[1:], jnp.full((1,), n_slots, jnp.int32)])
    pos_blk = jnp.arange(n_blocks, dtype=jnp.int32) + jnp.sum(exp_b[None, :] < blk_b[:, None], axis=1)
    pos_exp = jnp.arange(N_EXPERTS, dtype=jnp.int32) + jnp.sum(blk_b[None, :] <= exp_b[:, None], axis=1)
    slot = jnp.arange(n_blocks + N_EXPERTS, dtype=jnp.int32)[:, None]
    bounds = (jnp.sum(jnp.where(pos_blk[None, :] == slot, blk_b[None, :], 0), axis=1)
              + jnp.sum(jnp.where(pos_exp[None, :] == slot, exp_b[None, :], 0), axis=1))
    item_lo, item_hi = bounds[:-1], bounds[1:]
    item_blk = jnp.minimum(item_lo // bm, n_blocks - 1)
    item_e = jnp.clip(jnp.sum(start[None, :] <= item_lo[:, None], axis=1) - 1, 0, N_EXPERTS - 1).astype(jnp.int32)
    changed = jnp.concatenate([jnp.zeros((1,), jnp.int32), (item_e[1:] != item_e[:-1]).astype(jnp.int32)])
    item_slot = jnp.cumsum(changed).astype(jnp.int32) % 3

    def following(e):
        later = jnp.where(item_e[None, :] > e[:, None], item_e[None, :], N_EXPERTS)
        nxt = jnp.min(later, axis=1)
        return jnp.where(jnp.logical_or(nxt == N_EXPERTS, e < 0), -1, nxt).astype(jnp.int32)

    item_next = following(item_e)
    item_next2 = following(item_next)

    dest = _slots(start, eidx, rank)
    xs = _dispatch(h2p, dest, td)
    blk_ids = jnp.arange(n_blocks + 1, dtype=jnp.int32)
    item_first = jnp.sum(item_blk[None, :] < blk_ids[:, None], axis=1).astype(jnp.int32)
    ys = _experts((item_first, item_e, item_lo, item_hi, item_slot, item_next, item_next2), xs, w1[0], w3[0], w2[0],
                  bm)
    out = _combine(xp, wts.T, mod, g_final[None, :], dest, ys, n_lat, td)
    return out.reshape(n_batch, n_lat, d)
```
